```python
import jax, jax.numpy as jnp
from jax import lax
import numpy as np

D_MODEL = 2048
BATCH = 8
SEQ = 2048
DEPTH = 1

N_MEM = 256
EPS = 1e-6

GLA_HEADS = 4
GLA_DV = D_MODEL // 2
GLA_DK = GLA_DV // 2
GLA_HK = GLA_DK // GLA_HEADS
GLA_HV = GLA_DV // GLA_HEADS
GLA_GATE_RANK = 16
GLA_GATE_NORM = 16.0
GLA_CHUNK = 64

POOL_WIDTH = D_MODEL // 2
POOL_WINDOWS = (2, 4, 8, 16)
POOL_GROUPS = len(POOL_WINDOWS)
POOL_GW = POOL_WIDTH // POOL_GROUPS

N_BRANCH = 2

CROSS_HEADS = 4
CROSS_HD = D_MODEL // CROSS_HEADS

D_FF = 256 * ((8 * D_MODEL // 3 + 255) // 256)
CONV_W = 3

OFF_K = GLA_DK
OFF_V = 2 * GLA_DK
OFF_R = OFF_V + GLA_DV
OFF_A = OFF_R + GLA_DV
OFF_P = OFF_A + GLA_GATE_RANK
OFF_G = OFF_P + POOL_WIDTH
D_IN = OFF_G + N_BRANCH * D_MODEL

kernel_name = "gla_pool_gated_hybrid_block"


def rms_norm(x, g):
    xf = x.astype(jnp.float32)
    y = xf * lax.rsqrt(jnp.mean(xf * xf, axis=-1, keepdims=True) + EPS)
    return (y * g.astype(jnp.float32)).astype(x.dtype)


def gla_chunked(q, k, v, log_a):
    B, H, T, dk = q.shape
    dv = v.shape[-1]
    C = GLA_CHUNK
    n = T // C

    def to_chunks(t):
        return jnp.moveaxis(t.reshape(B, H, n, C, t.shape[-1]), 2, 0)

    qc, kc, vc, gc = to_chunks(q), to_chunks(k), to_chunks(v), to_chunks(log_a)
    causal = jnp.tril(jnp.ones((C, C), dtype=bool))[:, :, None]

    def step(S, inp):
        qi, ki, vi, gi = inp
        b = jnp.cumsum(gi, axis=2)
        diff = b[:, :, :, None, :] - b[:, :, None, :, :]
        decay = jnp.exp(jnp.where(causal, diff, -jnp.inf))
        A = jnp.einsum('bhid,bhjd,bhijd->bhij', qi, ki, decay)
        o = (jnp.einsum('bhij,bhjv->bhiv', A, vi)
             + jnp.einsum('bhid,bhdv->bhiv', qi * jnp.exp(b), S))
        b_last = b[:, :, -1:, :]
        S = (jnp.exp(b_last[:, :, 0, :])[..., None] * S
             + jnp.einsum('bhjd,bhjv->bhdv', ki * jnp.exp(b_last - b), vi))
        return S, o

    S0 = jnp.zeros((B, H, dk, dv), jnp.float32)
    _, o = lax.scan(step, S0, (qc, kc, vc, gc))
    return jnp.moveaxis(o, 0, 2).reshape(B, H, T, dv)


def multiscale_pool(p, w_pool, pool_scale):
    B, T, _ = p.shape
    pf = p.astype(jnp.float32)
    cs = jnp.concatenate([jnp.zeros((B, 1, POOL_WIDTH), jnp.float32),
                          jnp.cumsum(pf, axis=1)], axis=1)
    pos = jnp.arange(T)
    outs = []
    for gi, w in enumerate(POOL_WINDOWS):
        sl = slice(gi * POOL_GW, (gi + 1) * POOL_GW)
        start = jnp.maximum(pos + 1 - w, 0)
        cnt = (pos + 1 - start).astype(jnp.float32)
        window_sum = cs[:, 1:, sl] - cs[:, start, sl]
        outs.append(window_sum / cnt[None, :, None] - pf[:, :, sl])
    pooled = jnp.stack(outs, axis=2)
    mixed = jnp.einsum('btgc,gcd->btgd', pooled, w_pool.astype(jnp.float32))
    mixed = mixed.reshape(B, T, POOL_WIDTH) * pool_scale.astype(jnp.float32)
    return mixed.astype(p.dtype)


def hybrid_mixer(h, w_in, w_a2, b_a, g_gla, w_pool, pool_scale, w_branch, w_out):
    B, T, _ = h.shape
    f32 = jnp.float32
    proj = h @ w_in

    def heads(t, d):
        return t.reshape(B, T, GLA_HEADS, d).transpose(0, 2, 1, 3).astype(f32)

    q = heads(proj[..., :OFF_K], GLA_HK) * (GLA_HK ** -0.5)
    k = heads(proj[..., OFF_K:OFF_V], GLA_HK)
    v = heads(proj[..., OFF_V:OFF_R], GLA_HV)
    r = proj[..., OFF_R:OFF_A]
    gate_pre = (proj[..., OFF_A:OFF_P] @ w_a2 + b_a).astype(f32)
    log_a = heads(jax.nn.log_sigmoid(gate_pre) / GLA_GATE_NORM, GLA_HK)
    o = gla_chunked(q, k, v, log_a)
    o = o * lax.rsqrt(jnp.mean(o * o, axis=-1, keepdims=True) + EPS)
    o = o.transpose(0, 2, 1, 3).reshape(B, T, GLA_DV) * g_gla.astype(f32)
    o_gla = o.astype(h.dtype) * jax.nn.silu(r)

    o_pool = multiscale_pool(proj[..., OFF_P:OFF_G], w_pool, pool_scale)

    y_gla = o_gla @ w_branch[:GLA_DV]
    y_pool = o_pool @ w_branch[GLA_DV:]
    gates = jax.nn.sigmoid(proj[..., OFF_G:].astype(f32)).astype(h.dtype)
    merged = gates[..., :D_MODEL] * y_gla + gates[..., D_MODEL:] * y_pool
    return merged @ w_out


def memory_cross_attention(h, mem_n, w_cq, w_ckv, w_co):
    B, T, _ = h.shape
    M = mem_n.shape[1]
    q = (h @ w_cq).reshape(B, T, CROSS_HEADS, CROSS_HD)
    kv = (mem_n @ w_ckv).reshape(B, M, 2, CROSS_HEADS, CROSS_HD)
    k, v = kv[:, :, 0], kv[:, :, 1]
    s = jnp.einsum('bthd,bmhd->bhtm', q, k).astype(jnp.float32) * (CROSS_HD ** -0.5)
    pr = jax.nn.softmax(s, axis=-1).astype(v.dtype)
    o = jnp.einsum('bhtm,bmhd->bthd', pr, v).reshape(B, T, D_MODEL)
    return o @ w_co


def conv_glu_ffn(h, w_up, conv_w, conv_b, w_down):
    u = h @ w_up
    u = lax.conv_general_dilated(
        u, conv_w[:, None, :], window_strides=(1,), padding=[(CONV_W - 1, 0)],
        dimension_numbers=('NWC', 'WIO', 'NWC'), feature_group_count=2 * D_FF) + conv_b
    gate, val = u[..., :D_FF], u[..., D_FF:]
    return (jax.nn.silu(gate) * val) @ w_down


def setup_inputs(seed: int = 0) -> dict:
    key = jax.random.key(seed)
    ks = jax.random.split(key, 24)
    L, D = DEPTH, D_MODEL
    nrm = lambda k, shape, fan_in: jax.random.normal(k, shape, jnp.float32) * (fan_in ** -0.5)
    gain = lambda k, shape: 1.0 + 0.02 * jax.random.normal(k, shape, jnp.float32)
    return {
        "x": jax.random.normal(ks[0], (BATCH, SEQ, D), jnp.float32),
        "mem": jax.random.normal(ks[1], (BATCH, N_MEM, D), jnp.float32),
        "g_mix": gain(ks[2], (L, D)),
        "w_in": nrm(ks[3], (L, D, D_IN), D),
        "w_a2": nrm(ks[4], (L, GLA_GATE_RANK, GLA_DK), GLA_GATE_RANK),
        "b_a": 0.1 * jax.random.normal(ks[5], (L, GLA_DK), jnp.float32),
        "g_gla": gain(ks[6], (L, GLA_DV)),
        "w_pool": nrm(ks[7], (L, POOL_GROUPS, POOL_GW, POOL_GW), POOL_GW),
        "pool_scale": gain(ks[8], (L, POOL_WIDTH)),
        "w_branch": nrm(ks[9], (L, GLA_DV + POOL_WIDTH, D), GLA_DV),
        "w_out": nrm(ks[10], (L, D, D), D),
        "g_cross": gain(ks[11], (L, D)),
        "g_mem": gain(ks[12], (L, D)),
        "w_cq": nrm(ks[13], (L, D, D), D),
        "w_ckv": nrm(ks[14], (L, D, 2 * D), D),
        "w_co": nrm(ks[15], (L, D, D), D),
        "g_ffn": gain(ks[16], (L, D)),
        "w_up": nrm(ks[17], (L, D, 2 * D_FF), D),
        "conv_w": nrm(ks[18], (L, CONV_W, 2 * D_FF), CONV_W),
        "conv_b": 0.02 * jax.random.normal(ks[19], (L, 2 * D_FF), jnp.float32),
        "w_down": nrm(ks[20], (L, D_FF, D), D_FF),
        "g_final": gain(ks[21], (D,)),
    }


def reference(x, mem, g_mix, w_in, w_a2, b_a, g_gla, w_pool, pool_scale, w_branch, w_out,
              g_cross, g_mem, w_cq, w_ckv, w_co, g_ffn, w_up, conv_w, conv_b, w_down, g_final):
    for l in range(DEPTH):
        x = x + hybrid_mixer(rms_norm(x, g_mix[l]), w_in[l], w_a2[l], b_a[l], g_gla[l],
                             w_pool[l], pool_scale[l], w_branch[l], w_out[l])
        x = x + memory_cross_attention(rms_norm(x, g_cross[l]), rms_norm(mem, g_mem[l]),
                                       w_cq[l], w_ckv[l], w_co[l])
        x = x + conv_glu_ffn(rms_norm(x, g_ffn[l]), w_up[l], conv_w[l], conv_b[l], w_down[l])
    return rms_norm(x, g_final)
```

```python
import functools

import numpy as np
import jax
import jax.numpy as jnp
from jax import lax
from jax.experimental import pallas as pl
from jax.experimental.pallas import tpu as pltpu

F32 = jnp.float32
BF16 = jnp.bfloat16

EPS = 1e-6
GLA_HEADS = 4
GLA_GATE_RANK = 16
GLA_GATE_NORM = 16.0
GLA_CHUNK = 64
POOL_WINDOWS = (2, 4, 8, 16)
CROSS_HEADS = 4
CONV_W = 3

LANES = 128
VMEM_LIMIT = 52 * 1024 * 1024

GLA_LEVELS = (32, 16, 8, 4, 2, 1)


def _cparams(*sem):
    return pltpu.CompilerParams(dimension_semantics=sem, vmem_limit_bytes=VMEM_LIMIT)


def _const_spec(shape):
    nd = len(shape)
    return pl.BlockSpec(shape, lambda *_: (0,) * nd, pipeline_mode=pl.Buffered(1))


def _rms(xf, g):
    return xf * lax.rsqrt(jnp.mean(xf * xf, axis=-1, keepdims=True) + EPS) * g


def _sigmoid(x):
    return 1.0 / (1.0 + jnp.exp(-x))


def _dot(a, b):
    return jnp.dot(a, b, preferred_element_type=F32)


def _dot_nt(a, b):
    return lax.dot_general(a, b, (((1,), (1,)), ((), ())), preferred_element_type=F32)


def _dot_tn(a, b):
    return lax.dot_general(a, b, (((0,), (0,)), ((), ())), preferred_element_type=F32)


def _norm_gate_kernel(x_ref, g_ref, wa1_ref, wa2_ref, ba_ref, h_ref, la_ref):
    h = _rms(x_ref[...], g_ref[...]).astype(BF16)
    h_ref[...] = h
    a = _dot(h, wa1_ref[...])
    gp = _dot(a.astype(BF16), wa2_ref[...]) + ba_ref[...]
    ls = jnp.minimum(gp, 0.0) - jnp.log1p(jnp.exp(-jnp.abs(gp)))
    la_ref[...] = ls * (1.0 / GLA_GATE_NORM)


def _norm_gate(x2d, g, wa1, wa2, ba, tm=512):
    n, d = x2d.shape
    dk = wa2.shape[1]
    return pl.pallas_call(
        _norm_gate_kernel,
        grid=(n // tm,),
        in_specs=[
            pl.BlockSpec((tm, d), lambda i: (i, 0)),
            _const_spec((1, d)),
            _const_spec(wa1.shape),
            _const_spec(wa2.shape),
            _const_spec((1, dk)),
        ],
        out_specs=[
            pl.BlockSpec((tm, d), lambda i: (i, 0)),
            pl.BlockSpec((tm, dk), lambda i: (i, 0)),
        ],
        out_shape=[
            jax.ShapeDtypeStruct((n, d), BF16),
            jax.ShapeDtypeStruct((n, dk), F32),
        ],
        compiler_params=_cparams("parallel"),
        name="norm_gate",
    )(x2d, g, wa1, wa2, ba)


def _mm_kernel(a_ref, w_ref, o_ref):
    o_ref[...] = _dot(a_ref[...], w_ref[...]).astype(o_ref.dtype)


def _matmul(a, w, tm, tn, out_dtype, name):
    n, k = a.shape
    m = w.shape[1]
    return pl.pallas_call(
        _mm_kernel,
        grid=(n // tm, m // tn),
        in_specs=[
            pl.BlockSpec((tm, k), lambda i, j: (i, 0)),
            pl.BlockSpec((k, tn), lambda i, j: (0, j)),
        ],
        out_specs=pl.BlockSpec((tm, tn), lambda i, j: (i, j)),
        out_shape=jax.ShapeDtypeStruct((n, m), out_dtype),
        compiler_params=_cparams("parallel", "parallel"),
        name=name,
    )(a, w)


def _gla_consts():
    c = GLA_CHUNK
    i = np.arange(c)[:, None]
    t = np.arange(c)[None, :]
    blocks = [t <= i, t > i]
    masks = []
    for s in GLA_LEVELS:
        upper = (i % (2 * s)) >= s
        blk_start = i - (i % s)
        blk_end = blk_start + s - 1
        blocks.append((upper & (t >= blk_start) & (t <= i)) | (~upper & (t > i) & (t <= blk_end)))
        same_group = (i // (2 * s)) == (t // (2 * s))
        masks.append(same_group & upper & ((t % (2 * s)) < s))
    masks.append(i == t)
    assert (np.sum(masks, axis=0) == (t <= i)).all()
    expo = np.concatenate(blocks, axis=0).astype(np.float32)
    return jnp.asarray(expo, BF16), jnp.asarray(np.stack(masks).astype(np.float32))


def _gla_kernel(q_ref, k_ref, v_ref, r_ref, la_ref, gg_ref, expo_ref, mask_ref, o_ref, st_ref,
                *, seq, q_scale):
    c = GLA_CHUNK
    n_levels = len(GLA_LEVELS)
    st_ref[...] = jnp.zeros_like(st_ref)

    def chunk(ci, carry):
        r0 = pl.multiple_of(ci * c, c)
        rows = pl.ds(r0, c)
        g = la_ref[rows, :]
        g1 = g.astype(BF16)
        rem = g - g1.astype(F32)
        g2 = rem.astype(BF16)
        g3 = (rem - g2.astype(F32)).astype(BF16)
        expo = expo_ref[...]
        dec = jnp.exp(_dot(expo, g1) + _dot(expo, g2) + _dot(expo, g3))

        q = q_ref[rows, :].astype(F32) * q_scale
        k = k_ref[rows, :].astype(F32)
        v = v_ref[rows, :]

        att = mask_ref[n_levels] * _dot_nt(q.astype(BF16), k.astype(BF16))
        for l in range(n_levels):
            f = dec[(2 + l) * c:(3 + l) * c]
            att = att + mask_ref[l] * _dot_nt((q * f).astype(BF16), (k * f).astype(BF16))

        q_in = (q * dec[0:c]).astype(BF16)
        k_out = (k * dec[c:2 * c]).astype(BF16)
        st = st_ref[...]
        o = _dot(att.astype(BF16), v) + _dot_nt(q_in, st.astype(BF16))
        st_ref[...] = st * dec[c - 1:c] + _dot_tn(v, k_out)

        on = o * lax.rsqrt(jnp.mean(o * o, axis=-1, keepdims=True) + EPS) * gg_ref[...]
        r = r_ref[rows, :].astype(F32)
        o_ref[rows, :] = (on * (r * _sigmoid(r))).astype(o_ref.dtype)
        return carry

    lax.fori_loop(0, seq // c, chunk, 0)


def _gla(proj, log_a, g_gla, batch, seq, dk_total, dv_total):
    n = proj.shape[0]
    hk = dk_total // GLA_HEADS
    hv = dv_total // GLA_HEADS
    expo, masks = _gla_consts()
    k_blk = dk_total // hk
    v_blk = 2 * dk_total // hv
    r_blk = (2 * dk_total + dv_total) // hv
    return pl.pallas_call(
        functools.partial(_gla_kernel, seq=seq, q_scale=hk ** -0.5),
        grid=(batch, GLA_HEADS),
        in_specs=[
            pl.BlockSpec((seq, hk), lambda b, h: (b, h)),
            pl.BlockSpec((seq, hk), lambda b, h: (b, k_blk + h)),
            pl.BlockSpec((seq, hv), lambda b, h: (b, v_blk + h)),
            pl.BlockSpec((seq, hv), lambda b, h: (b, r_blk + h)),
            pl.BlockSpec((seq, hk), lambda b, h: (b, h)),
            pl.BlockSpec((1, hv), lambda b, h: (0, h)),
            _const_spec(expo.shape),
            _const_spec(masks.shape),
        ],
        out_specs=pl.BlockSpec((seq, hv), lambda b, h: (b, h)),
        out_shape=jax.ShapeDtypeStruct((n, dv_total), BF16),
        scratch_shapes=[pltpu.VMEM((hv, hk), F32)],
        compiler_params=_cparams("parallel", "parallel"),
        name="gla",
    )(proj, proj, proj, proj, log_a, g_gla, expo, masks)


def _pool_kernel(p_ref, w_ref, sc_ref, o_ref, *, seq, gw):
    row = lax.broadcasted_iota(jnp.int32, (seq, gw), 0)
    row1 = lax.broadcasted_iota(jnp.int32, (seq, 1), 0)
    for gi, w in enumerate(POOL_WINDOWS):
        cols = slice(gi * gw, (gi + 1) * gw)
        x = p_ref[:, cols].astype(F32)
        acc = x
        d = 1
        while d < w:
            acc = acc + jnp.where(row >= d, pltpu.roll(acc, d, 0), 0.0)
            d *= 2
        inv_cnt = 1.0 / jnp.minimum(row1 + 1, w).astype(F32)
        pooled = acc * inv_cnt - x
        mixed = _dot(pooled.astype(BF16), w_ref[gi]) * sc_ref[:, cols]
        o_ref[:, cols] = mixed.astype(o_ref.dtype)


def _pool(proj, w_pool, pool_scale, batch, seq, col_off):
    n = proj.shape[0]
    ng, gw, _ = w_pool.shape
    width = ng * gw
    return pl.pallas_call(
        functools.partial(_pool_kernel, seq=seq, gw=gw),
        grid=(batch,),
        in_specs=[
            pl.BlockSpec((seq, width), lambda b: (b, col_off // width)),
            _const_spec(w_pool.shape),
            _const_spec((1, width)),
        ],
        out_specs=pl.BlockSpec((seq, width), lambda b: (b, 0)),
        out_shape=jax.ShapeDtypeStruct((n, width), BF16),
        compiler_params=_cparams("parallel"),
        name="pool",
    )(proj, w_pool, pool_scale)


def _merge_kernel(og_ref, op_ref, g1_ref, g2_ref, x_ref, wb_ref, wo_ref, gn_ref, x1_ref, h_ref):
    dv = og_ref.shape[1]
    yg = _dot(og_ref[...], wb_ref[0:dv, :])
    yp = _dot(op_ref[...], wb_ref[dv:, :])
    merged = _sigmoid(g1_ref[...].astype(F32)) * yg + _sigmoid(g2_ref[...].astype(F32)) * yp
    x1 = x_ref[...] + _dot(merged.astype(BF16), wo_ref[...])
    x1_ref[...] = x1
    h_ref[...] = _rms(x1, gn_ref[...]).astype(h_ref.dtype)


def _merge(o_gla, o_pool, proj, x2d, w_branch, w_out, g_next, gate_off, tm=256):
    n, d = x2d.shape
    dv = o_gla.shape[1]
    dp = o_pool.shape[1]
    gb = gate_off // d
    return pl.pallas_call(
        _merge_kernel,
        grid=(n // tm,),
        in_specs=[
            pl.BlockSpec((tm, dv), lambda i: (i, 0)),
            pl.BlockSpec((tm, dp), lambda i: (i, 0)),
            pl.BlockSpec((tm, d), lambda i: (i, gb)),
            pl.BlockSpec((tm, d), lambda i: (i, gb + 1)),
            pl.BlockSpec((tm, d), lambda i: (i, 0)),
            _const_spec(w_branch.shape),
            _const_spec(w_out.shape),
            _const_spec((1, d)),
        ],
        out_specs=[
            pl.BlockSpec((tm, d), lambda i: (i, 0)),
            pl.BlockSpec((tm, d), lambda i: (i, 0)),
        ],
        out_shape=[
            jax.ShapeDtypeStruct((n, d), F32),
            jax.ShapeDtypeStruct((n, d), BF16),
        ],
        compiler_params=_cparams("parallel"),
        name="merge",
    )(o_gla, o_pool, proj, proj, x2d, w_branch, w_out, g_next)


def _norm_mm_kernel(x_ref, g_ref, w_ref, o_ref):
    h = _rms(x_ref[...], g_ref[...]).astype(BF16)
    o_ref[...] = _dot(h, w_ref[...]).astype(o_ref.dtype)


def _norm_matmul(x2d, g, w, tm, tn, name):
    n, d = x2d.shape
    m = w.shape[1]
    return pl.pallas_call(
        _norm_mm_kernel,
        grid=(n // tm, m // tn),
        in_specs=[
            pl.BlockSpec((tm, d), lambda i, j: (i, 0)),
            _const_spec((1, d)),
            pl.BlockSpec((d, tn), lambda i, j: (0, j)),
        ],
        out_specs=pl.BlockSpec((tm, tn), lambda i, j: (i, j)),
        out_shape=jax.ShapeDtypeStruct((n, m), BF16),
        compiler_params=_cparams("parallel", "parallel"),
        name=name,
    )(x2d, g, w)


def _cross_kernel(h_ref, kv_ref, x_ref, wq_ref, wo_ref, gn_ref, x2_ref, hn_ref):
    d = h_ref.shape[1]
    hd = d // CROSS_HEADS
    q = _dot(h_ref[...], wq_ref[...]).astype(BF16)
    heads = []
    for hh in range(CROSS_HEADS):
        cols = slice(hh * hd, (hh + 1) * hd)
        s = _dot_nt(q[:, cols], kv_ref[:, cols]) * (hd ** -0.5)
        p = jnp.exp(s - jnp.max(s, axis=-1, keepdims=True))
        p = p * (1.0 / jnp.sum(p, axis=-1, keepdims=True))
        heads.append(_dot(p.astype(BF16), kv_ref[:, d + hh * hd:d + (hh + 1) * hd]).astype(BF16))
    o = jnp.concatenate(heads, axis=1)
    x2 = x_ref[...] + _dot(o, wo_ref[...])
    x2_ref[...] = x2
    hn_ref[...] = _rms(x2, gn_ref[...]).astype(hn_ref.dtype)


def _cross(h, kv, x2d, w_cq, w_co, g_next, batch, seq, n_mem, tm=256):
    n, d = x2d.shape
    spb = seq // tm
    return pl.pallas_call(
        _cross_kernel,
        grid=(batch, spb),
        in_specs=[
            pl.BlockSpec((tm, d), lambda b, i: (b * spb + i, 0)),
            pl.BlockSpec((n_mem, 2 * d), lambda b, i: (b, 0)),
            pl.BlockSpec((tm, d), lambda b, i: (b * spb + i, 0)),
            _const_spec(w_cq.shape),
            _const_spec(w_co.shape),
            _const_spec((1, d)),
        ],
        out_specs=[
            pl.BlockSpec((tm, d), lambda b, i: (b * spb + i, 0)),
            pl.BlockSpec((tm, d), lambda b, i: (b * spb + i, 0)),
        ],
        out_shape=[
            jax.ShapeDtypeStruct((n, d), F32),
            jax.ShapeDtypeStruct((n, d), BF16),
        ],
        compiler_params=_cparams("parallel", "parallel"),
        name="cross",
    )(h, kv, x2d, w_cq, w_co, g_next)


HALO = 8


def _ffn_up_kernel(h_ref, wg_ref, wv_ref, cwg_ref, cwv_ref, cbg_ref, cbv_ref, act_ref,
                   halo_g, halo_v, *, tiles_per_seq):
    tm = h_ref.shape[0]
    h = h_ref[...]
    row = lax.broadcasted_iota(jnp.int32, (tm, act_ref.shape[1]), 0)

    @pl.when((pl.program_id(1) % tiles_per_seq) == 0)
    def _():
        halo_g[...] = jnp.zeros_like(halo_g)
        halo_v[...] = jnp.zeros_like(halo_v)

    def conv(w_ref, cw_ref, cb_ref, halo_ref):
        u = _dot(h, w_ref[...])
        prev = halo_ref[...]
        u1 = jnp.where(row == 0, prev[HALO - 1:HALO], pltpu.roll(u, 1, 0))
        u2 = jnp.where(row == 0, prev[HALO - 2:HALO - 1],
                       jnp.where(row == 1, prev[HALO - 1:HALO], pltpu.roll(u, 2, 0)))
        halo_ref[...] = u[tm - HALO:tm]
        return cw_ref[2:3] * u + cw_ref[1:2] * u1 + cw_ref[0:1] * u2 + cb_ref[...]

    gate = conv(wg_ref, cwg_ref, cbg_ref, halo_g)
    val = conv(wv_ref, cwv_ref, cbv_ref, halo_v)
    act_ref[...] = (gate * _sigmoid(gate) * val).astype(act_ref.dtype)


def _ffn_up(h, w_up, conv_w, conv_b, seq, tm=1024, tn=512):
    n, d = h.shape
    dff = w_up.shape[1] // 2
    nj = dff // tn
    return pl.pallas_call(
        functools.partial(_ffn_up_kernel, tiles_per_seq=seq // tm),
        grid=(nj, n // tm),
        in_specs=[
            pl.BlockSpec((tm, d), lambda j, i: (i, 0)),
            pl.BlockSpec((d, tn), lambda j, i: (0, j)),
            pl.BlockSpec((d, tn), lambda j, i: (0, nj + j)),
            pl.BlockSpec((CONV_W, tn), lambda j, i: (0, j)),
            pl.BlockSpec((CONV_W, tn), lambda j, i: (0, nj + j)),
            pl.BlockSpec((1, tn), lambda j, i: (0, j)),
            pl.BlockSpec((1, tn), lambda j, i: (0, nj + j)),
        ],
        out_specs=pl.BlockSpec((tm, tn), lambda j, i: (i, j)),
        out_shape=jax.ShapeDtypeStruct((n, dff), BF16),
        scratch_shapes=[pltpu.VMEM((HALO, tn), F32), pltpu.VMEM((HALO, tn), F32)],
        compiler_params=_cparams("arbitrary", "arbitrary"),
        name="ffn_up",
    )(h, w_up, w_up, conv_w, conv_w, conv_b, conv_b)


def _ffn_down_kernel(a_ref, w_ref, x_ref, g_ref, o_ref, acc_ref):
    kk = pl.program_id(1)

    @pl.when(kk == 0)
    def _():
        acc_ref[...] = x_ref[...]

    acc_ref[...] += _dot(a_ref[...], w_ref[...])

    @pl.when(kk == pl.num_programs(1) - 1)
    def _():
        o_ref[...] = _rms(acc_ref[...], g_ref[...])


def _ffn_down(act, w_down, x2d, g, tm=512, tk=1408):
    n, d = x2d.shape
    dff = act.shape[1]
    return pl.pallas_call(
        _ffn_down_kernel,
        grid=(n // tm, dff // tk),
        in_specs=[
            pl.BlockSpec((tm, tk), lambda i, kk: (i, kk)),
            pl.BlockSpec((tk, d), lambda i, kk: (kk, 0)),
            pl.BlockSpec((tm, d), lambda i, kk: (i, 0)),
            _const_spec((1, d)),
        ],
        out_specs=pl.BlockSpec((tm, d), lambda i, kk: (i, 0)),
        out_shape=jax.ShapeDtypeStruct((n, d), F32),
        scratch_shapes=[pltpu.VMEM((tm, d), F32)],
        compiler_params=_cparams("parallel", "arbitrary"),
        name="ffn_down",
    )(act, w_down, x2d, g)


def kernel(x, mem, g_mix, w_in, w_a2, b_a, g_gla, w_pool, pool_scale, w_branch, w_out,
           g_cross, g_mem, w_cq, w_ckv, w_co, g_ffn, w_up, conv_w, conv_b, w_down, g_final):
    batch, seq, d = x.shape
    n_mem = mem.shape[1]
    depth = w_in.shape[0]
    dk = w_a2.shape[2]
    dv = g_gla.shape[1]
    pw = pool_scale.shape[1]
    off_a = 2 * dk + 2 * dv
    off_p = off_a + GLA_GATE_RANK
    row2 = lambda v: v.reshape(1, -1)

    xs = x.reshape(batch * seq, d)
    mem2d = mem.reshape(batch * n_mem, d)

    out = None
    for l in range(depth):
        w_main = jnp.concatenate([w_in[l][:, :off_a], w_in[l][:, off_p:]], axis=1).astype(BF16)
        wa1 = jnp.pad(w_in[l][:, off_a:off_p], ((0, 0), (0, LANES - GLA_GATE_RANK))).astype(BF16)
        wa2 = jnp.pad(w_a2[l], ((0, LANES - GLA_GATE_RANK), (0, 0))).astype(BF16)
        g_after_ffn = g_mix[l + 1] if l + 1 < depth else g_final

        h, log_a = _norm_gate(xs, row2(g_mix[l]), wa1, wa2, row2(b_a[l]))
        proj = _matmul(h, w_main, 1024, 1024, BF16, "in_proj")
        o_gla = _gla(proj, log_a, row2(g_gla[l]), batch, seq, dk, dv)
        o_pool = _pool(proj, w_pool[l].astype(BF16), row2(pool_scale[l]), batch, seq, off_a)
        x1, h2 = _merge(o_gla, o_pool, proj, xs, w_branch[l].astype(BF16), w_out[l].astype(BF16),
                        row2(g_cross[l]), off_a + pw)
        kv = _norm_matmul(mem2d, row2(g_mem[l]), w_ckv[l].astype(BF16), 512, 1024, "mem_kv")
        x2, h3 = _cross(h2, kv, x1, w_cq[l].astype(BF16), w_co[l].astype(BF16), row2(g_ffn[l]),
                        batch, seq, n_mem)
        act = _ffn_up(h3, w_up[l].astype(BF16), conv_w[l], row2(conv_b[l]), seq)
        assert depth == 1
        out = _ffn_down(act, w_down[l].astype(BF16), x2, row2(g_after_ffn))
    return out.reshape(batch, seq, d)
```

```python
import functools

import numpy as np
import jax
import jax.numpy as jnp
from jax import lax
from jax.experimental import pallas as pl
from jax.experimental.pallas import tpu as pltpu

F32 = jnp.float32
BF16 = jnp.bfloat16

EPS = 1e-6
GLA_HEADS = 4
GLA_GATE_RANK = 16
GLA_GATE_NORM = 16.0
GLA_CHUNK = 64
POOL_WINDOWS = (2, 4, 8, 16)
CROSS_HEADS = 4
CONV_W = 3

LANES = 128
SUBLANES = 8
VMEM_LIMIT = 52 * 1024 * 1024

GLA_LEVELS = (32, 16, 8, 4, 2, 1)
GLA_SPLIT = 3


def _cparams(*sem):
    return pltpu.CompilerParams(dimension_semantics=sem, vmem_limit_bytes=VMEM_LIMIT)


def _const_spec(shape):
    nd = len(shape)
    return pl.BlockSpec(shape, lambda *_: (0,) * nd, pipeline_mode=pl.Buffered(1))


def _rms(xf, g):
    return xf * lax.rsqrt(jnp.mean(xf * xf, axis=-1, keepdims=True) + EPS) * g


def _sigmoid(x):
    return 1.0 / (1.0 + jnp.exp(-x))


def _dot(a, b):
    return jnp.dot(a, b, preferred_element_type=F32)


def _dot_nt(a, b):
    return lax.dot_general(a, b, (((1,), (1,)), ((), ())), preferred_element_type=F32)


def _dot_tn(a, b):
    return lax.dot_general(a, b, (((0,), (0,)), ((), ())), preferred_element_type=F32)


def _norm_gate_kernel(x_ref, g_ref, wa1_ref, wa2_ref, ba_ref, h_ref, la_ref):
    h = _rms(x_ref[...], g_ref[...]).astype(BF16)
    h_ref[...] = h
    a = _dot(h, wa1_ref[...])
    gp = _dot(a.astype(BF16), wa2_ref[...]) + ba_ref[...]
    ls = jnp.minimum(gp, 0.0) - jnp.log1p(jnp.exp(-jnp.abs(gp)))
    la_ref[...] = ls * (1.0 / GLA_GATE_NORM)


def _norm_gate(x2d, g, wa1, wa2, ba, tm=512):
    n, d = x2d.shape
    dk = wa2.shape[1]
    return pl.pallas_call(
        _norm_gate_kernel,
        grid=(n // tm,),
        in_specs=[
            pl.BlockSpec((tm, d), lambda i: (i, 0)),
            _const_spec((1, d)),
            _const_spec(wa1.shape),
            _const_spec(wa2.shape),
            _const_spec((1, dk)),
        ],
        out_specs=[
            pl.BlockSpec((tm, d), lambda i: (i, 0)),
            pl.BlockSpec((tm, dk), lambda i: (i, 0)),
        ],
        out_shape=[
            jax.ShapeDtypeStruct((n, d), BF16),
            jax.ShapeDtypeStruct((n, dk), F32),
        ],
        compiler_params=_cparams("parallel"),
        name="norm_gate",
    )(x2d, g, wa1, wa2, ba)


def _mm_kernel(a_ref, w_ref, o_ref):
    o_ref[...] = _dot(a_ref[...], w_ref[...]).astype(o_ref.dtype)


def _matmul(a, w, tm, tn, out_dtype, name):
    n, k = a.shape
    m = w.shape[1]
    return pl.pallas_call(
        _mm_kernel,
        grid=(n // tm, m // tn),
        in_specs=[
            pl.BlockSpec((tm, k), lambda i, j: (i, 0)),
            pl.BlockSpec((k, tn), lambda i, j: (0, j)),
        ],
        out_specs=pl.BlockSpec((tm, tn), lambda i, j: (i, j)),
        out_shape=jax.ShapeDtypeStruct((n, m), out_dtype),
        compiler_params=_cparams("parallel", "parallel"),
        name=name,
    )(a, w)


def _gla_consts():
    c = GLA_CHUNK
    i = np.arange(c)[:, None]
    t = np.arange(c)[None, :]
    blocks = [t <= i, t > i]
    masks = []
    for s in GLA_LEVELS:
        upper = (i % (2 * s)) >= s
        blk_start = i - (i % s)
        blk_end = blk_start + s - 1
        blocks.append((upper & (t >= blk_start) & (t <= i)) | (~upper & (t > i) & (t <= blk_end)))
        same_group = (i // (2 * s)) == (t // (2 * s))
        masks.append(same_group & upper & ((t % (2 * s)) < s))
    masks.append(i == t)
    assert (np.sum(masks, axis=0) == (t <= i)).all()
    expo = np.concatenate(blocks, axis=0).astype(np.float32)
    expo = np.concatenate([expo] * GLA_SPLIT, axis=1)
    return jnp.asarray(expo, BF16), jnp.asarray(np.stack(masks).astype(np.float32))


def _gla_kernel(q_ref, k_ref, v_ref, r_ref, la_ref, gg_ref, expo_ref, mask_ref, o_ref, st_ref,
                *, q_scale):
    c = GLA_CHUNK
    n_levels = len(GLA_LEVELS)
    hk = q_ref.shape[1] // GLA_HEADS
    hv = v_ref.shape[1] // GLA_HEADS

    @pl.when(pl.program_id(1) == 0)
    def _():
        st_ref[...] = jnp.zeros_like(st_ref)

    def chunk(ci, carry):
        r0 = pl.multiple_of(ci * c, c)
        rows = pl.ds(r0, c)
        g = la_ref[rows, :]
        g1 = g.astype(BF16)
        rem = g - g1.astype(F32)
        g2 = rem.astype(BF16)
        g3 = (rem - g2.astype(F32)).astype(BF16)
        dec_all = jnp.exp(_dot(expo_ref[...], jnp.concatenate([g1, g2, g3], axis=0)))
        for hh in range(GLA_HEADS):
            kc = slice(hh * hk, (hh + 1) * hk)
            vc = slice(hh * hv, (hh + 1) * hv)
            dec = dec_all[:, kc]

            q = q_ref[rows, kc].astype(F32) * q_scale
            k = k_ref[rows, kc].astype(F32)
            v = v_ref[rows, vc]

            att = mask_ref[n_levels] * _dot_nt(q.astype(BF16), k.astype(BF16))
            for l in range(n_levels):
                f = dec[(2 + l) * c:(3 + l) * c]
                att = att + mask_ref[l] * _dot_nt((q * f).astype(BF16), (k * f).astype(BF16))

            q_in = (q * dec[0:c]).astype(BF16)
            k_out = (k * dec[c:2 * c]).astype(BF16)
            st = st_ref[hh]
            o = _dot(att.astype(BF16), v) + _dot_nt(q_in, st.astype(BF16))
            st_ref[hh] = st * dec[c - 1:c] + _dot_tn(v, k_out)

            on = o * lax.rsqrt(jnp.mean(o * o, axis=-1, keepdims=True) + EPS) * gg_ref[:, vc]
            r = r_ref[rows, vc].astype(F32)
            o_ref[rows, vc] = (on * (r * _sigmoid(r))).astype(o_ref.dtype)
        return carry

    lax.fori_loop(0, q_ref.shape[0] // c, chunk, 0, unroll=2)


def _gla(proj, log_a, g_gla, batch, seq, dk, dv, tt=1024):
    n = proj.shape[0]
    hk = dk // GLA_HEADS
    hv = dv // GLA_HEADS
    expo, masks = _gla_consts()
    spb = seq // tt
    v_blk = 2 * dk // dv
    return pl.pallas_call(
        functools.partial(_gla_kernel, q_scale=hk ** -0.5),
        grid=(batch, spb),
        in_specs=[
            pl.BlockSpec((tt, dk), lambda b, s: (b * spb + s, 0)),
            pl.BlockSpec((tt, dk), lambda b, s: (b * spb + s, 1)),
            pl.BlockSpec((tt, dv), lambda b, s: (b * spb + s, v_blk)),
            pl.BlockSpec((tt, dv), lambda b, s: (b * spb + s, v_blk + 1)),
            pl.BlockSpec((tt, dk), lambda b, s: (b * spb + s, 0)),
            _const_spec((1, dv)),
            _const_spec(expo.shape),
            _const_spec(masks.shape),
        ],
        out_specs=pl.BlockSpec((tt, dv), lambda b, s: (b * spb + s, 0)),
        out_shape=jax.ShapeDtypeStruct((n, dv), BF16),
        scratch_shapes=[pltpu.VMEM((GLA_HEADS, hv, hk), F32)],
        compiler_params=_cparams("parallel", "arbitrary"),
        name="gla",
    )(proj, proj, proj, proj, log_a, g_gla, expo, masks)


def _pool_kernel(p_ref, w_ref, sc_ref, o_ref, *, seq, gw):
    row = lax.broadcasted_iota(jnp.int32, (seq, gw), 0)
    row1 = lax.broadcasted_iota(jnp.int32, (seq, 1), 0)
    for gi, w in enumerate(POOL_WINDOWS):
        cols = slice(gi * gw, (gi + 1) * gw)
        x = p_ref[:, cols].astype(F32)
        acc = x
        d = 1
        while d < w:
            acc = acc + jnp.where(row >= d, pltpu.roll(acc, d, 0), 0.0)
            d *= 2
        inv_cnt = 1.0 / jnp.minimum(row1 + 1, w).astype(F32)
        pooled = acc * inv_cnt - x
        mixed = _dot(pooled.astype(BF16), w_ref[gi]) * sc_ref[:, cols]
        o_ref[:, cols] = mixed.astype(o_ref.dtype)


def _pool(proj, w_pool, pool_scale, batch, seq):
    n = proj.shape[0]
    ng, gw, _ = w_pool.shape
    width = ng * gw
    return pl.pallas_call(
        functools.partial(_pool_kernel, seq=seq, gw=gw),
        grid=(batch,),
        in_specs=[
            pl.BlockSpec((seq, width), lambda b: (b, 0)),
            _const_spec(w_pool.shape),
            _const_spec((1, width)),
        ],
        out_specs=pl.BlockSpec((seq, width), lambda b: (b, 0)),
        out_shape=jax.ShapeDtypeStruct((n, width), BF16),
        compiler_params=_cparams("parallel"),
        name="pool",
    )(proj, w_pool, pool_scale)


def _merge_kernel(og_ref, op_ref, pg_ref, x_ref, wb_ref, wo_ref, gn_ref, x1_ref, h_ref):
    dv = og_ref.shape[1]
    d = x_ref.shape[1]
    g0 = pg_ref.shape[1] - 2 * d
    yg = _dot(og_ref[...], wb_ref[0:dv, :])
    yp = _dot(op_ref[...], wb_ref[dv:, :])
    merged = (_sigmoid(pg_ref[:, g0:g0 + d].astype(F32)) * yg
              + _sigmoid(pg_ref[:, g0 + d:].astype(F32)) * yp)
    x1 = x_ref[...] + _dot(merged.astype(BF16), wo_ref[...])
    x1_ref[...] = x1
    h_ref[...] = _rms(x1, gn_ref[...]).astype(h_ref.dtype)


def _merge(o_gla, o_pool, proj_pg, x2d, w_branch, w_out, g_next, tm=256):
    n, d = x2d.shape
    dv = o_gla.shape[1]
    dp = o_pool.shape[1]
    wpg = proj_pg.shape[1]
    return pl.pallas_call(
        _merge_kernel,
        grid=(n // tm,),
        in_specs=[
            pl.BlockSpec((tm, dv), lambda i: (i, 0)),
            pl.BlockSpec((tm, dp), lambda i: (i, 0)),
            pl.BlockSpec((tm, wpg), lambda i: (i, 0)),
            pl.BlockSpec((tm, d), lambda i: (i, 0)),
            _const_spec(w_branch.shape),
            _const_spec(w_out.shape),
            _const_spec((1, d)),
        ],
        out_specs=[
            pl.BlockSpec((tm, d), lambda i: (i, 0)),
            pl.BlockSpec((tm, d), lambda i: (i, 0)),
        ],
        out_shape=[
            jax.ShapeDtypeStruct((n, d), F32),
            jax.ShapeDtypeStruct((n, d), BF16),
        ],
        compiler_params=_cparams("parallel"),
        name="merge",
    )(o_gla, o_pool, proj_pg, x2d, w_branch, w_out, g_next)


def _norm_mm_kernel(x_ref, g_ref, w_ref, o_ref):
    h = _rms(x_ref[...], g_ref[...]).astype(BF16)
    o_ref[...] = _dot(h, w_ref[...]).astype(o_ref.dtype)


def _norm_matmul(x2d, g, w, tm, tn, name):
    n, d = x2d.shape
    m = w.shape[1]
    return pl.pallas_call(
        _norm_mm_kernel,
        grid=(n // tm, m // tn),
        in_specs=[
            pl.BlockSpec((tm, d), lambda i, j: (i, 0)),
            _const_spec((1, d)),
            pl.BlockSpec((d, tn), lambda i, j: (0, j)),
        ],
        out_specs=pl.BlockSpec((tm, tn), lambda i, j: (i, j)),
        out_shape=jax.ShapeDtypeStruct((n, m), BF16),
        compiler_params=_cparams("parallel", "parallel"),
        name=name,
    )(x2d, g, w)


def _cross_kernel(h_ref, kv_ref, x_ref, wq_ref, wo_ref, gn_ref, x2_ref, hn_ref):
    d = h_ref.shape[1]
    hd = d // CROSS_HEADS
    q = _dot(h_ref[...], wq_ref[...]).astype(BF16)
    heads = []
    for hh in range(CROSS_HEADS):
        cols = slice(hh * hd, (hh + 1) * hd)
        s = _dot_nt(q[:, cols], kv_ref[:, cols]) * (hd ** -0.5)
        p = jnp.exp(s - jnp.max(s, axis=-1, keepdims=True))
        p = p * (1.0 / jnp.sum(p, axis=-1, keepdims=True))
        heads.append(_dot(p.astype(BF16), kv_ref[:, d + hh * hd:d + (hh + 1) * hd]).astype(BF16))
    o = jnp.concatenate(heads, axis=1)
    x2 = x_ref[...] + _dot(o, wo_ref[...])
    x2_ref[...] = x2
    hn_ref[...] = _rms(x2, gn_ref[...]).astype(hn_ref.dtype)


def _cross(h, kv, x2d, w_cq, w_co, g_next, batch, seq, n_mem, tm=256):
    n, d = x2d.shape
    spb = seq // tm
    return pl.pallas_call(
        _cross_kernel,
        grid=(batch, spb),
        in_specs=[
            pl.BlockSpec((tm, d), lambda b, i: (b * spb + i, 0)),
            pl.BlockSpec((n_mem, 2 * d), lambda b, i: (b, 0)),
            pl.BlockSpec((tm, d), lambda b, i: (b * spb + i, 0)),
            _const_spec(w_cq.shape),
            _const_spec(w_co.shape),
            _const_spec((1, d)),
        ],
        out_specs=[
            pl.BlockSpec((tm, d), lambda b, i: (b * spb + i, 0)),
            pl.BlockSpec((tm, d), lambda b, i: (b * spb + i, 0)),
        ],
        out_shape=[
            jax.ShapeDtypeStruct((n, d), F32),
            jax.ShapeDtypeStruct((n, d), BF16),
        ],
        compiler_params=_cparams("parallel", "parallel"),
        name="cross",
    )(h, kv, x2d, w_cq, w_co, g_next)


def _ffn_up_kernel(h_ref, wg_ref, wv_ref, cwg_ref, cwv_ref, cbg_ref, cbv_ref, act_ref, *, rb):
    seq, tn = act_ref.shape
    row = lax.broadcasted_iota(jnp.int32, (SUBLANES, tn), 0)

    def conv(u, prev, cw_ref, cb_ref):
        r1 = pltpu.roll(u, 1, 0)
        r2 = pltpu.roll(u, 2, 0)
        last = prev[SUBLANES - 1:SUBLANES]
        top1 = jnp.where(row == 0, last, r1[0:SUBLANES])
        top2 = jnp.where(row == 0, prev[SUBLANES - 2:SUBLANES - 1],
                         jnp.where(row == 1, last, r2[0:SUBLANES]))
        u1 = jnp.concatenate([top1, r1[SUBLANES:]], axis=0)
        u2 = jnp.concatenate([top2, r2[SUBLANES:]], axis=0)
        return cw_ref[2:3] * u + cw_ref[1:2] * u1 + cw_ref[0:1] * u2 + cb_ref[...]

    prev_g = jnp.zeros((SUBLANES, tn), F32)
    prev_v = jnp.zeros((SUBLANES, tn), F32)
    for r0 in range(0, seq, rb):
        hb = h_ref[r0:r0 + rb, :]
        ug = _dot(hb, wg_ref[...])
        uv = _dot(hb, wv_ref[...])
        gate = conv(ug, prev_g, cwg_ref, cbg_ref)
        val = conv(uv, prev_v, cwv_ref, cbv_ref)
        prev_g = ug[rb - SUBLANES:]
        prev_v = uv[rb - SUBLANES:]
        act_ref[r0:r0 + rb, :] = (gate * _sigmoid(gate) * val).astype(act_ref.dtype)


def _ffn_up(h, w_up, conv_w, conv_b, batch, seq, tn=512, rb=512):
    n, d = h.shape
    dff = w_up.shape[1] // 2
    nj = dff // tn
    return pl.pallas_call(
        functools.partial(_ffn_up_kernel, rb=rb),
        grid=(batch, nj),
        in_specs=[
            pl.BlockSpec((seq, d), lambda b, j: (b, 0)),
            pl.BlockSpec((d, tn), lambda b, j: (0, j)),
            pl.BlockSpec((d, tn), lambda b, j: (0, nj + j)),
            pl.BlockSpec((CONV_W, tn), lambda b, j: (0, j)),
            pl.BlockSpec((CONV_W, tn), lambda b, j: (0, nj + j)),
            pl.BlockSpec((1, tn), lambda b, j: (0, j)),
            pl.BlockSpec((1, tn), lambda b, j: (0, nj + j)),
        ],
        out_specs=pl.BlockSpec((seq, tn), lambda b, j: (b, j)),
        out_shape=jax.ShapeDtypeStruct((n, dff), BF16),
        compiler_params=_cparams("parallel", "parallel"),
        name="ffn_up",
    )(h, w_up, w_up, conv_w, conv_w, conv_b, conv_b)


def _ffn_down_kernel(a_ref, w_ref, x_ref, g_ref, o_ref):
    o_ref[...] = _rms(x_ref[...] + _dot(a_ref[...], w_ref[...]), g_ref[...])


def _ffn_down(act, w_down, x2d, g, tm=256):
    n, d = x2d.shape
    dff = act.shape[1]
    return pl.pallas_call(
        _ffn_down_kernel,
        grid=(n // tm,),
        in_specs=[
            pl.BlockSpec((tm, dff), lambda i: (i, 0)),
            _const_spec(w_down.shape),
            pl.BlockSpec((tm, d), lambda i: (i, 0)),
            _const_spec((1, d)),
        ],
        out_specs=pl.BlockSpec((tm, d), lambda i: (i, 0)),
        out_shape=jax.ShapeDtypeStruct((n, d), F32),
        compiler_params=_cparams("parallel"),
        name="ffn_down",
    )(act, w_down, x2d, g)


def kernel(x, mem, g_mix, w_in, w_a2, b_a, g_gla, w_pool, pool_scale, w_branch, w_out,
           g_cross, g_mem, w_cq, w_ckv, w_co, g_ffn, w_up, conv_w, conv_b, w_down, g_final):
    batch, seq, d = x.shape
    n_mem = mem.shape[1]
    depth = w_in.shape[0]
    assert depth == 1
    dk = w_a2.shape[2]
    dv = g_gla.shape[1]
    off_a = 2 * dk + 2 * dv
    off_p = off_a + GLA_GATE_RANK
    row2 = lambda v: v.reshape(1, -1)

    xs = x.reshape(batch * seq, d)
    mem2d = mem.reshape(batch * n_mem, d)

    w_qkvr = w_in[0][:, :off_a].astype(BF16)
    w_pg = w_in[0][:, off_p:].astype(BF16)
    wa1 = w_in[0][:, off_a:off_a + LANES].astype(BF16)
    wa2 = jnp.pad(w_a2[0], ((0, LANES - GLA_GATE_RANK), (0, 0))).astype(BF16)

    h, log_a = _norm_gate(xs, row2(g_mix[0]), wa1, wa2, row2(b_a[0]))
    proj_qkvr = _matmul(h, w_qkvr, 1024, 1024, BF16, "in_proj_qkvr")
    proj_pg = _matmul(h, w_pg, 1024, 1024, BF16, "in_proj_pg")
    o_gla = _gla(proj_qkvr, log_a, row2(g_gla[0]), batch, seq, dk, dv)
    o_pool = _pool(proj_pg, w_pool[0].astype(BF16), row2(pool_scale[0]), batch, seq)
    x1, h2 = _merge(o_gla, o_pool, proj_pg, xs, w_branch[0].astype(BF16), w_out[0].astype(BF16),
                    row2(g_cross[0]))
    kv = _norm_matmul(mem2d, row2(g_mem[0]), w_ckv[0].astype(BF16), 512, 1024, "mem_kv")
    x2, h3 = _cross(h2, kv, x1, w_cq[0].astype(BF16), w_co[0].astype(BF16), row2(g_ffn[0]),
                    batch, seq, n_mem)
    act = _ffn_up(h3, w_up[0].astype(BF16), conv_w[0], row2(conv_b[0]), batch, seq)
    out = _ffn_down(act, w_down[0].astype(BF16), x2, row2(g_final))
    return out.reshape(batch, seq, d)
```

```python
import functools

import numpy as np
import jax
import jax.numpy as jnp
from jax import lax
from jax.experimental import pallas as pl
from jax.experimental.pallas import tpu as pltpu

F32 = jnp.float32
BF16 = jnp.bfloat16

EPS = 1e-6
LOG2_E = 1.4426950408889634
GLA_HEADS = 4
GLA_GATE_RANK = 16
GLA_GATE_NORM = 16.0
GLA_CHUNK = 64
POOL_WINDOWS = (2, 4, 8, 16)
CROSS_HEADS = 4
CONV_W = 3

LANES = 128
SUBLANES = 8
VMEM_LIMIT = 52 * 1024 * 1024

GLA_LEVELS = (32, 16, 8, 4, 2, 1)
GLA_SPLIT = 3


def _cparams(*sem):
    return pltpu.CompilerParams(dimension_semantics=sem, vmem_limit_bytes=VMEM_LIMIT)


def _const_spec(shape):
    nd = len(shape)
    return pl.BlockSpec(shape, lambda *_: (0,) * nd, pipeline_mode=pl.Buffered(1))


def _rms(xf, g):
    return xf * lax.rsqrt(jnp.mean(xf * xf, axis=-1, keepdims=True) + EPS) * g


def _sigmoid(x):
    return 1.0 / (1.0 + jnp.exp(-x))


def _dot(a, b):
    return jnp.dot(a, b, preferred_element_type=F32)


def _dot_nt(a, b):
    return lax.dot_general(a, b, (((1,), (1,)), ((), ())), preferred_element_type=F32)


def _dot_tn(a, b):
    return lax.dot_general(a, b, (((0,), (0,)), ((), ())), preferred_element_type=F32)


def _norm_gate_kernel(x_ref, g_ref, wa1_ref, wa2_ref, ba_ref, h_ref, la_ref):
    h = _rms(x_ref[...], g_ref[...]).astype(BF16)
    h_ref[...] = h
    a = _dot_nt(h, wa1_ref[...].astype(BF16))
    gp = _dot(a.astype(BF16), wa2_ref[...]) + ba_ref[...]
    ls = jnp.minimum(gp, 0.0) - jnp.log1p(jnp.exp(-jnp.abs(gp)))
    la_ref[...] = ls * (1.0 / GLA_GATE_NORM)


def _norm_gate(x2d, g, w_in_t, gate_row0, wa2, ba, tm=512):
    n, d = x2d.shape
    dk = wa2.shape[1]
    return pl.pallas_call(
        _norm_gate_kernel,
        grid=(n // tm,),
        in_specs=[
            pl.BlockSpec((tm, d), lambda i: (i, 0)),
            _const_spec((1, d)),
            pl.BlockSpec((LANES, d), lambda i: (gate_row0 // LANES, 0), pipeline_mode=pl.Buffered(1)),
            _const_spec(wa2.shape),
            _const_spec((1, dk)),
        ],
        out_specs=[
            pl.BlockSpec((tm, d), lambda i: (i, 0)),
            pl.BlockSpec((tm, dk), lambda i: (i, 0)),
        ],
        out_shape=[
            jax.ShapeDtypeStruct((n, d), BF16),
            jax.ShapeDtypeStruct((n, dk), F32),
        ],
        compiler_params=_cparams("parallel"),
        name="norm_gate",
    )(x2d, g, w_in_t, wa2, ba)


def _mm_nt_kernel(a_ref, wt_ref, o_ref, wbf_ref):
    @pl.when(pl.program_id(1) == 0)
    def _():
        wbf_ref[...] = wt_ref[...].astype(BF16)

    o_ref[...] = _dot_nt(a_ref[...], wbf_ref[...]).astype(o_ref.dtype)


def _matmul_nt(a, wt, row0, m, tm, tn, out_dtype, name):
    n, k = a.shape
    return pl.pallas_call(
        _mm_nt_kernel,
        grid=(m // tn, n // tm),
        in_specs=[
            pl.BlockSpec((tm, k), lambda j, i: (i, 0)),
            pl.BlockSpec((pl.Element(tn), pl.Element(k)),
                         lambda j, i: (pl.multiple_of(row0 + j * tn, SUBLANES), 0)),
        ],
        out_specs=pl.BlockSpec((tm, tn), lambda j, i: (i, j)),
        out_shape=jax.ShapeDtypeStruct((n, m), out_dtype),
        scratch_shapes=[pltpu.VMEM((tn, k), BF16)],
        compiler_params=_cparams("parallel", "arbitrary"),
        name=name,
    )(a, wt)


def _gla_consts():
    c = GLA_CHUNK
    i = np.arange(c)[:, None]
    t = np.arange(c)[None, :]
    blocks = [t <= i, t > i]
    masks = []
    for s in GLA_LEVELS:
        upper = (i % (2 * s)) >= s
        blk_start = i - (i % s)
        blk_end = blk_start + s - 1
        blocks.append((upper & (t >= blk_start) & (t <= i)) | (~upper & (t > i) & (t <= blk_end)))
        same_group = (i // (2 * s)) == (t // (2 * s))
        masks.append(same_group & upper & ((t % (2 * s)) < s))
    masks.append(i == t)
    assert (np.sum(masks, axis=0) == (t <= i)).all()
    expo = np.concatenate(blocks, axis=0).astype(np.float32)
    expo = np.concatenate([expo] * GLA_SPLIT, axis=1)
    return jnp.asarray(expo, BF16), jnp.asarray(np.stack(masks).astype(np.float32))


def _gla_kernel(q_ref, k_ref, v_ref, r_ref, la_ref, gg_ref, expo_ref, mask_ref, o_ref, st_ref,
                *, q_scale):
    c = GLA_CHUNK
    n_levels = len(GLA_LEVELS)
    hk = q_ref.shape[1] // GLA_HEADS
    hv = v_ref.shape[1] // GLA_HEADS

    @pl.when(pl.program_id(1) == 0)
    def _():
        st_ref[...] = jnp.zeros_like(st_ref)

    def chunk(ci, carry):
        r0 = pl.multiple_of(ci * c, c)
        rows = pl.ds(r0, c)
        g = la_ref[rows, :] * LOG2_E
        g1 = g.astype(BF16)
        rem = g - g1.astype(F32)
        g2 = rem.astype(BF16)
        g3 = (rem - g2.astype(F32)).astype(BF16)
        dec_all = jnp.exp2(_dot(expo_ref[...], jnp.concatenate([g1, g2, g3], axis=0)))
        for hh in range(GLA_HEADS):
            kc = slice(hh * hk, (hh + 1) * hk)
            vc = slice(hh * hv, (hh + 1) * hv)
            dec = dec_all[:, kc]

            q = q_ref[rows, kc].astype(F32) * q_scale
            k = k_ref[rows, kc].astype(F32)
            v = v_ref[rows, vc]

            att = mask_ref[n_levels] * _dot_nt(q.astype(BF16), k.astype(BF16))
            for l in range(n_levels):
                f = dec[(2 + l) * c:(3 + l) * c]
                att = att + mask_ref[l] * _dot_nt((q * f).astype(BF16), (k * f).astype(BF16))

            q_in = (q * dec[0:c]).astype(BF16)
            k_out = (k * dec[c:2 * c]).astype(BF16)
            st = st_ref[hh]
            o = _dot(att.astype(BF16), v) + _dot_nt(q_in, st.astype(BF16))
            st_ref[hh] = st * dec[c - 1:c] + _dot_tn(v, k_out)

            on = o * lax.rsqrt(jnp.mean(o * o, axis=-1, keepdims=True) + EPS) * gg_ref[:, vc]
            r = r_ref[rows, vc].astype(F32)
            o_ref[rows, vc] = (on * (r * _sigmoid(r))).astype(o_ref.dtype)
        return carry

    lax.fori_loop(0, q_ref.shape[0] // c, chunk, 0, unroll=4)


def _gla(proj, log_a, g_gla, batch, seq, dk, dv, tt=1024):
    n = proj.shape[0]
    hk = dk // GLA_HEADS
    hv = dv // GLA_HEADS
    expo, masks = _gla_consts()
    spb = seq // tt
    v_blk = 2 * dk // dv
    return pl.pallas_call(
        functools.partial(_gla_kernel, q_scale=hk ** -0.5),
        grid=(batch, spb),
        in_specs=[
            pl.BlockSpec((tt, dk), lambda b, s: (b * spb + s, 0)),
            pl.BlockSpec((tt, dk), lambda b, s: (b * spb + s, 1)),
            pl.BlockSpec((tt, dv), lambda b, s: (b * spb + s, v_blk)),
            pl.BlockSpec((tt, dv), lambda b, s: (b * spb + s, v_blk + 1)),
            pl.BlockSpec((tt, dk), lambda b, s: (b * spb + s, 0)),
            _const_spec((1, dv)),
            _const_spec(expo.shape),
            _const_spec(masks.shape),
        ],
        out_specs=pl.BlockSpec((tt, dv), lambda b, s: (b * spb + s, 0)),
        out_shape=jax.ShapeDtypeStruct((n, dv), BF16),
        scratch_shapes=[pltpu.VMEM((GLA_HEADS, hv, hk), F32)],
        compiler_params=_cparams("parallel", "arbitrary"),
        name="gla",
    )(proj, proj, proj, proj, log_a, g_gla, expo, masks)


def _pool_kernel(p_ref, w_ref, sc_ref, o_ref, *, seq, gw):
    row = lax.broadcasted_iota(jnp.int32, (seq, gw), 0)
    row1 = lax.broadcasted_iota(jnp.int32, (seq, 1), 0)
    for gi, w in enumerate(POOL_WINDOWS):
        cols = slice(gi * gw, (gi + 1) * gw)
        x = p_ref[:, cols].astype(F32)
        acc = x
        d = 1
        while d < w:
            acc = acc + jnp.where(row >= d, pltpu.roll(acc, d, 0), 0.0)
            d *= 2
        inv_cnt = 1.0 / jnp.minimum(row1 + 1, w).astype(F32)
        pooled = acc * inv_cnt - x
        mixed = _dot(pooled.astype(BF16), w_ref[gi]) * sc_ref[:, cols]
        o_ref[:, cols] = mixed.astype(o_ref.dtype)


def _pool(proj, w_pool, pool_scale, batch, seq):
    n = proj.shape[0]
    ng, gw, _ = w_pool.shape
    width = ng * gw
    return pl.pallas_call(
        functools.partial(_pool_kernel, seq=seq, gw=gw),
        grid=(batch,),
        in_specs=[
            pl.BlockSpec((seq, width), lambda b: (b, 0)),
            _const_spec(w_pool.shape),
            _const_spec((1, width)),
        ],
        out_specs=pl.BlockSpec((seq, width), lambda b: (b, 0)),
        out_shape=jax.ShapeDtypeStruct((n, width), BF16),
        compiler_params=_cparams("parallel"),
        name="pool",
    )(proj, w_pool, pool_scale)


def _merge_kernel(og_ref, op_ref, pg_ref, x_ref, wb_ref, wo_ref, gn_ref, x1_ref, h_ref):
    dv = og_ref.shape[1]
    d = x_ref.shape[1]
    g0 = pg_ref.shape[1] - 2 * d
    yg = _dot(og_ref[...], wb_ref[0:dv, :])
    yp = _dot(op_ref[...], wb_ref[dv:, :])
    merged = (_sigmoid(pg_ref[:, g0:g0 + d].astype(F32)) * yg
              + _sigmoid(pg_ref[:, g0 + d:].astype(F32)) * yp)
    x1 = x_ref[...] + _dot(merged.astype(BF16), wo_ref[...])
    x1_ref[...] = x1
    h_ref[...] = _rms(x1, gn_ref[...]).astype(h_ref.dtype)


def _merge(o_gla, o_pool, proj_pg, x2d, w_branch, w_out, g_next, tm=256):
    n, d = x2d.shape
    dv = o_gla.shape[1]
    dp = o_pool.shape[1]
    wpg = proj_pg.shape[1]
    return pl.pallas_call(
        _merge_kernel,
        grid=(n // tm,),
        in_specs=[
            pl.BlockSpec((tm, dv), lambda i: (i, 0)),
            pl.BlockSpec((tm, dp), lambda i: (i, 0)),
            pl.BlockSpec((tm, wpg), lambda i: (i, 0)),
            pl.BlockSpec((tm, d), lambda i: (i, 0)),
            _const_spec(w_branch.shape),
            _const_spec(w_out.shape),
            _const_spec((1, d)),
        ],
        out_specs=[
            pl.BlockSpec((tm, d), lambda i: (i, 0)),
            pl.BlockSpec((tm, d), lambda i: (i, 0)),
        ],
        out_shape=[
            jax.ShapeDtypeStruct((n, d), F32),
            jax.ShapeDtypeStruct((n, d), BF16),
        ],
        compiler_params=_cparams("parallel"),
        name="merge",
    )(o_gla, o_pool, proj_pg, x2d, w_branch, w_out, g_next)


def _norm_mm_kernel(x_ref, g_ref, w_ref, o_ref, wbf_ref):
    @pl.when(pl.program_id(1) == 0)
    def _():
        wbf_ref[...] = w_ref[...].astype(BF16)

    h = _rms(x_ref[...], g_ref[...]).astype(BF16)
    o_ref[...] = _dot(h, wbf_ref[...]).astype(o_ref.dtype)


def _norm_matmul(x2d, g, w, tm, tn, name):
    n, d = x2d.shape
    m = w.shape[1]
    return pl.pallas_call(
        _norm_mm_kernel,
        grid=(m // tn, n // tm),
        in_specs=[
            pl.BlockSpec((tm, d), lambda j, i: (i, 0)),
            _const_spec((1, d)),
            pl.BlockSpec((d, tn), lambda j, i: (0, j)),
        ],
        out_specs=pl.BlockSpec((tm, tn), lambda j, i: (i, j)),
        out_shape=jax.ShapeDtypeStruct((n, m), BF16),
        scratch_shapes=[pltpu.VMEM((d, tn), BF16)],
        compiler_params=_cparams("parallel", "arbitrary"),
        name=name,
    )(x2d, g, w)


def _cross_kernel(h_ref, kv_ref, x_ref, wq_ref, wo_ref, gn_ref, x2_ref, hn_ref):
    d = h_ref.shape[1]
    hd = d // CROSS_HEADS
    q = _dot(h_ref[...], wq_ref[...]).astype(BF16)
    heads = []
    for hh in range(CROSS_HEADS):
        cols = slice(hh * hd, (hh + 1) * hd)
        s = _dot_nt(q[:, cols], kv_ref[:, cols]) * (hd ** -0.5)
        p = jnp.exp(s - jnp.max(s, axis=-1, keepdims=True))
        p = p * (1.0 / jnp.sum(p, axis=-1, keepdims=True))
        heads.append(_dot(p.astype(BF16), kv_ref[:, d + hh * hd:d + (hh + 1) * hd]).astype(BF16))
    o = jnp.concatenate(heads, axis=1)
    x2 = x_ref[...] + _dot(o, wo_ref[...])
    x2_ref[...] = x2
    hn_ref[...] = _rms(x2, gn_ref[...]).astype(hn_ref.dtype)


def _cross(h, kv, x2d, w_cq, w_co, g_next, batch, seq, n_mem, tm=256):
    n, d = x2d.shape
    spb = seq // tm
    return pl.pallas_call(
        _cross_kernel,
        grid=(batch, spb),
        in_specs=[
            pl.BlockSpec((tm, d), lambda b, i: (b * spb + i, 0)),
            pl.BlockSpec((n_mem, 2 * d), lambda b, i: (b, 0)),
            pl.BlockSpec((tm, d), lambda b, i: (b * spb + i, 0)),
            _const_spec(w_cq.shape),
            _const_spec(w_co.shape),
            _const_spec((1, d)),
        ],
        out_specs=[
            pl.BlockSpec((tm, d), lambda b, i: (b * spb + i, 0)),
            pl.BlockSpec((tm, d), lambda b, i: (b * spb + i, 0)),
        ],
        out_shape=[
            jax.ShapeDtypeStruct((n, d), F32),
            jax.ShapeDtypeStruct((n, d), BF16),
        ],
        compiler_params=_cparams("parallel", "parallel"),
        name="cross",
    )(h, kv, x2d, w_cq, w_co, g_next)


def _ffn_up_kernel(h_ref, wg32_ref, wv32_ref, cwg_ref, cwv_ref, cbg_ref, cbv_ref, act_ref,
                   wg_ref, wv_ref, *, rb):
    seq, tn = act_ref.shape
    row = lax.broadcasted_iota(jnp.int32, (SUBLANES, tn), 0)

    @pl.when(pl.program_id(1) == 0)
    def _():
        wg_ref[...] = wg32_ref[...].astype(BF16)
        wv_ref[...] = wv32_ref[...].astype(BF16)

    def conv(u, prev, cw_ref, cb_ref):
        r1 = pltpu.roll(u, 1, 0)
        r2 = pltpu.roll(u, 2, 0)
        last = prev[SUBLANES - 1:SUBLANES]
        top1 = jnp.where(row == 0, last, r1[0:SUBLANES])
        top2 = jnp.where(row == 0, prev[SUBLANES - 2:SUBLANES - 1],
                         jnp.where(row == 1, last, r2[0:SUBLANES]))
        u1 = jnp.concatenate([top1, r1[SUBLANES:]], axis=0)
        u2 = jnp.concatenate([top2, r2[SUBLANES:]], axis=0)
        return cw_ref[2:3] * u + cw_ref[1:2] * u1 + cw_ref[0:1] * u2 + cb_ref[...]

    prev_g = jnp.zeros((SUBLANES, tn), F32)
    prev_v = jnp.zeros((SUBLANES, tn), F32)
    for r0 in range(0, seq, rb):
        hb = h_ref[r0:r0 + rb, :]
        ug = _dot(hb, wg_ref[...])
        uv = _dot(hb, wv_ref[...])
        gate = conv(ug, prev_g, cwg_ref, cbg_ref)
        val = conv(uv, prev_v, cwv_ref, cbv_ref)
        prev_g = ug[rb - SUBLANES:]
        prev_v = uv[rb - SUBLANES:]
        act_ref[r0:r0 + rb, :] = (gate * _sigmoid(gate) * val).astype(act_ref.dtype)


def _ffn_up(h, w_up, conv_w, conv_b, batch, seq, tn=512, rb=256):
    n, d = h.shape
    dff = w_up.shape[1] // 2
    nj = dff // tn
    return pl.pallas_call(
        functools.partial(_ffn_up_kernel, rb=rb),
        grid=(nj, batch),
        in_specs=[
            pl.BlockSpec((seq, d), lambda j, b: (b, 0)),
            pl.BlockSpec((d, tn), lambda j, b: (0, j)),
            pl.BlockSpec((d, tn), lambda j, b: (0, nj + j)),
            pl.BlockSpec((CONV_W, tn), lambda j, b: (0, j)),
            pl.BlockSpec((CONV_W, tn), lambda j, b: (0, nj + j)),
            pl.BlockSpec((1, tn), lambda j, b: (0, j)),
            pl.BlockSpec((1, tn), lambda j, b: (0, nj + j)),
        ],
        out_specs=pl.BlockSpec((seq, tn), lambda j, b: (b, j)),
        out_shape=jax.ShapeDtypeStruct((n, dff), BF16),
        scratch_shapes=[pltpu.VMEM((d, tn), BF16), pltpu.VMEM((d, tn), BF16)],
        compiler_params=_cparams("parallel", "arbitrary"),
        name="ffn_up",
    )(h, w_up, w_up, conv_w, conv_w, conv_b, conv_b)


def _ffn_down_kernel(a_ref, w_ref, x_ref, g_ref, o_ref):
    o_ref[...] = _rms(x_ref[...] + _dot(a_ref[...], w_ref[...]), g_ref[...])


def _ffn_down(act, w_down, x2d, g, tm=256):
    n, d = x2d.shape
    dff = act.shape[1]
    return pl.pallas_call(
        _ffn_down_kernel,
        grid=(n // tm,),
        in_specs=[
            pl.BlockSpec((tm, dff), lambda i: (i, 0)),
            _const_spec(w_down.shape),
            pl.BlockSpec((tm, d), lambda i: (i, 0)),
            _const_spec((1, d)),
        ],
        out_specs=pl.BlockSpec((tm, d), lambda i: (i, 0)),
        out_shape=jax.ShapeDtypeStruct((n, d), F32),
        compiler_params=_cparams("parallel"),
        name="ffn_down",
    )(act, w_down, x2d, g)


def kernel(x, mem, g_mix, w_in, w_a2, b_a, g_gla, w_pool, pool_scale, w_branch, w_out,
           g_cross, g_mem, w_cq, w_ckv, w_co, g_ffn, w_up, conv_w, conv_b, w_down, g_final):
    batch, seq, d = x.shape
    n_mem = mem.shape[1]
    depth = w_in.shape[0]
    assert depth == 1
    dk = w_a2.shape[2]
    dv = g_gla.shape[1]
    off_a = 2 * dk + 2 * dv
    off_p = off_a + GLA_GATE_RANK
    row2 = lambda v: v.reshape(1, -1)

    xs = x.reshape(batch * seq, d)
    mem2d = mem.reshape(batch * n_mem, d)

    w_in_t = jnp.swapaxes(w_in[0], 0, 1)
    d_in = w_in_t.shape[0]
    wa2 = jnp.pad(w_a2[0], ((0, LANES - GLA_GATE_RANK), (0, 0))).astype(BF16)

    h, log_a = _norm_gate(xs, row2(g_mix[0]), w_in_t, off_a, wa2, row2(b_a[0]))
    proj_qkvr = _matmul_nt(h, w_in_t, 0, off_a, 1024, 1024, BF16, "in_proj_qkvr")
    proj_pg = _matmul_nt(h, w_in_t, off_p, d_in - off_p, 1024, 1024, BF16, "in_proj_pg")
    o_gla = _gla(proj_qkvr, log_a, row2(g_gla[0]), batch, seq, dk, dv)
    o_pool = _pool(proj_pg, w_pool[0].astype(BF16), row2(pool_scale[0]), batch, seq)
    x1, h2 = _merge(o_gla, o_pool, proj_pg, xs, w_branch[0].astype(BF16), w_out[0].astype(BF16),
                    row2(g_cross[0]))
    kv = _norm_matmul(mem2d, row2(g_mem[0]), w_ckv[0], 512, 1024, "mem_kv")
    x2, h3 = _cross(h2, kv, x1, w_cq[0].astype(BF16), w_co[0].astype(BF16), row2(g_ffn[0]),
                    batch, seq, n_mem)
    act = _ffn_up(h3, w_up[0], conv_w[0], row2(conv_b[0]), batch, seq)
    out = _ffn_down(act, w_down[0].astype(BF16), x2, row2(g_final))
    return out.reshape(batch, seq, d)
```

```python
import functools

import numpy as np
import jax
import jax.numpy as jnp
from jax import lax
from jax.experimental import pallas as pl
from jax.experimental.pallas import tpu as pltpu

F32 = jnp.float32
BF16 = jnp.bfloat16

EPS = 1e-6
LOG2_E = 1.4426950408889634
GLA_HEADS = 4
GLA_GATE_RANK = 16
GLA_GATE_NORM = 16.0
GLA_CHUNK = 64
POOL_WINDOWS = (2, 4, 8, 16)
CROSS_HEADS = 4
CONV_W = 3

LANES = 128
SUBLANES = 8
VMEM_LIMIT = 52 * 1024 * 1024

FFN_BLOCK = 256
FFN_GROUPS = FFN_BLOCK // SUBLANES

GLA_LEVELS = (32, 16, 8, 4, 2, 1)
GLA_SPLIT = 3


def _cparams(*sem):
    return pltpu.CompilerParams(dimension_semantics=sem, vmem_limit_bytes=VMEM_LIMIT)


def _const_spec(shape):
    nd = len(shape)
    return pl.BlockSpec(shape, lambda *_: (0,) * nd, pipeline_mode=pl.Buffered(1))


def _rms(xf, g):
    return xf * lax.rsqrt(jnp.mean(xf * xf, axis=-1, keepdims=True) + EPS) * g


def _sigmoid(x):
    return 1.0 / (1.0 + jnp.exp(-x))


def _store_lane_chunks(ref, value):
    for c in range(ref.shape[0]):
        ref[c] = value[:, c * LANES:(c + 1) * LANES]


def _load_strided_rows(ref, first, stride):
    return jnp.concatenate([ref[c, pl.ds(first, SUBLANES, stride=stride), :]
                            for c in range(ref.shape[0])], axis=1)


def _dot(a, b):
    return jnp.dot(a, b, preferred_element_type=F32)


def _dot_nt(a, b):
    return lax.dot_general(a, b, (((1,), (1,)), ((), ())), preferred_element_type=F32)


def _dot_tn(a, b):
    return lax.dot_general(a, b, (((0,), (0,)), ((), ())), preferred_element_type=F32)


def _norm_gate_kernel(x_ref, g_ref, wa1_ref, wa2_ref, ba_ref, h_ref, la_ref):
    h = _rms(x_ref[...], g_ref[...]).astype(BF16)
    h_ref[...] = h
    a = _dot_nt(h, wa1_ref[...].astype(BF16))
    gp = _dot(a.astype(BF16), wa2_ref[...]) + ba_ref[...]
    ls = jnp.minimum(gp, 0.0) - jnp.log1p(jnp.exp(-jnp.abs(gp)))
    la_ref[...] = ls * (1.0 / GLA_GATE_NORM)


def _norm_gate(x2d, g, w_in_t, gate_row0, wa2, ba, tm=1024):
    n, d = x2d.shape
    dk = wa2.shape[1]
    return pl.pallas_call(
        _norm_gate_kernel,
        grid=(n // tm,),
        in_specs=[
            pl.BlockSpec((tm, d), lambda i: (i, 0)),
            _const_spec((1, d)),
            pl.BlockSpec((LANES, d), lambda i: (gate_row0 // LANES, 0), pipeline_mode=pl.Buffered(1)),
            _const_spec(wa2.shape),
            _const_spec((1, dk)),
        ],
        out_specs=[
            pl.BlockSpec((tm, d), lambda i: (i, 0)),
            pl.BlockSpec((tm, dk), lambda i: (i, 0)),
        ],
        out_shape=[
            jax.ShapeDtypeStruct((n, d), BF16),
            jax.ShapeDtypeStruct((n, dk), F32),
        ],
        compiler_params=_cparams("parallel"),
        name="norm_gate",
    )(x2d, g, w_in_t, wa2, ba)


def _mm_nt_kernel(a_ref, wt_ref, o_ref, wbf_ref):
    @pl.when(pl.program_id(1) == 0)
    def _():
        wbf_ref[...] = wt_ref[...].astype(BF16)

    o_ref[...] = _dot_nt(a_ref[...], wbf_ref[...]).astype(o_ref.dtype)


def _matmul_nt(a, wt, row0, m, tm, tn, out_dtype, name):
    n, k = a.shape
    return pl.pallas_call(
        _mm_nt_kernel,
        grid=(m // tn, n // tm),
        in_specs=[
            pl.BlockSpec((tm, k), lambda j, i: (i, 0)),
            pl.BlockSpec((pl.Element(tn), pl.Element(k)),
                         lambda j, i: (pl.multiple_of(row0 + j * tn, SUBLANES), 0)),
        ],
        out_specs=pl.BlockSpec((tm, tn), lambda j, i: (i, j)),
        out_shape=jax.ShapeDtypeStruct((n, m), out_dtype),
        scratch_shapes=[pltpu.VMEM((tn, k), BF16)],
        compiler_params=_cparams("parallel", "arbitrary"),
        name=name,
    )(a, wt)


def _gla_consts():
    c = GLA_CHUNK
    i = np.arange(c)[:, None]
    t = np.arange(c)[None, :]
    blocks = [t <= i, t > i]
    masks = []
    for s in GLA_LEVELS:
        upper = (i % (2 * s)) >= s
        blk_start = i - (i % s)
        blk_end = blk_start + s - 1
        blocks.append((upper & (t >= blk_start) & (t <= i)) | (~upper & (t > i) & (t <= blk_end)))
        same_group = (i // (2 * s)) == (t // (2 * s))
        masks.append(same_group & upper & ((t % (2 * s)) < s))
    masks.append(i == t)
    assert (np.sum(masks, axis=0) == (t <= i)).all()
    expo = np.concatenate(blocks, axis=0).astype(np.float32)
    expo = np.concatenate([expo] * GLA_SPLIT, axis=1)
    return jnp.asarray(expo, BF16), jnp.asarray(np.stack(masks).astype(np.float32))


def _gla_kernel(q_ref, k_ref, v_ref, r_ref, la_ref, gg_ref, expo_ref, mask_ref, o_ref, st_ref,
                *, q_scale):
    c = GLA_CHUNK
    n_levels = len(GLA_LEVELS)
    hk = q_ref.shape[1] // GLA_HEADS
    hv = v_ref.shape[1] // GLA_HEADS

    @pl.when(pl.program_id(1) == 0)
    def _():
        st_ref[...] = jnp.zeros_like(st_ref)

    def chunk(ci, carry):
        r0 = pl.multiple_of(ci * c, c)
        rows = pl.ds(r0, c)
        g = la_ref[rows, :] * LOG2_E
        g1 = g.astype(BF16)
        rem = g - g1.astype(F32)
        g2 = rem.astype(BF16)
        g3 = (rem - g2.astype(F32)).astype(BF16)
        dec_all = jnp.exp2(_dot(expo_ref[...], jnp.concatenate([g1, g2, g3], axis=0)))
        for hh in range(GLA_HEADS):
            kc = slice(hh * hk, (hh + 1) * hk)
            vc = slice(hh * hv, (hh + 1) * hv)
            dec = dec_all[:, kc]

            q = q_ref[rows, kc].astype(F32) * q_scale
            k = k_ref[rows, kc].astype(F32)
            v = v_ref[rows, vc]

            att = mask_ref[n_levels] * _dot_nt(q.astype(BF16), k.astype(BF16))
            for l in range(n_levels):
                f = dec[(2 + l) * c:(3 + l) * c]
                att = att + mask_ref[l] * _dot_nt((q * f).astype(BF16), (k * f).astype(BF16))

            q_in = (q * dec[0:c]).astype(BF16)
            k_out = (k * dec[c:2 * c]).astype(BF16)
            st = st_ref[hh]
            o = _dot(att.astype(BF16), v) + _dot_nt(q_in, st.astype(BF16))
            st_ref[hh] = st * dec[c - 1:c] + _dot_tn(v, k_out)

            on = o * lax.rsqrt(jnp.mean(o * o, axis=-1, keepdims=True) + EPS) * gg_ref[:, vc]
            r = r_ref[rows, vc].astype(F32)
            o_ref[rows, vc] = (on * (r * _sigmoid(r))).astype(o_ref.dtype)
        return carry

    lax.fori_loop(0, q_ref.shape[0] // c, chunk, 0, unroll=8)


def _gla(proj, log_a, g_gla, batch, seq, dk, dv, tt=1024):
    n = proj.shape[0]
    hk = dk // GLA_HEADS
    hv = dv // GLA_HEADS
    expo, masks = _gla_consts()
    spb = seq // tt
    v_blk = 2 * dk // dv
    return pl.pallas_call(
        functools.partial(_gla_kernel, q_scale=hk ** -0.5),
        grid=(batch, spb),
        in_specs=[
            pl.BlockSpec((tt, dk), lambda b, s: (b * spb + s, 0)),
            pl.BlockSpec((tt, dk), lambda b, s: (b * spb + s, 1)),
            pl.BlockSpec((tt, dv), lambda b, s: (b * spb + s, v_blk)),
            pl.BlockSpec((tt, dv), lambda b, s: (b * spb + s, v_blk + 1)),
            pl.BlockSpec((tt, dk), lambda b, s: (b * spb + s, 0)),
            _const_spec((1, dv)),
            _const_spec(expo.shape),
            _const_spec(masks.shape),
        ],
        out_specs=pl.BlockSpec((tt, dv), lambda b, s: (b * spb + s, 0)),
        out_shape=jax.ShapeDtypeStruct((n, dv), BF16),
        scratch_shapes=[pltpu.VMEM((GLA_HEADS, hv, hk), F32)],
        compiler_params=_cparams("parallel", "arbitrary"),
        name="gla",
    )(proj, proj, proj, proj, log_a, g_gla, expo, masks)


def _pool_kernel(p_ref, w_ref, sc_ref, o_ref, *, seq, gw):
    row = lax.broadcasted_iota(jnp.int32, (seq, gw), 0)
    row1 = lax.broadcasted_iota(jnp.int32, (seq, 1), 0)
    for gi, w in enumerate(POOL_WINDOWS):
        cols = slice(gi * gw, (gi + 1) * gw)
        x = p_ref[:, cols].astype(F32)
        acc = x
        d = 1
        while d < w:
            acc = acc + jnp.where(row >= d, pltpu.roll(acc, d, 0), 0.0)
            d *= 2
        inv_cnt = 1.0 / jnp.minimum(row1 + 1, w).astype(F32)
        pooled = acc * inv_cnt - x
        mixed = _dot(pooled.astype(BF16), w_ref[gi]) * sc_ref[:, cols]
        o_ref[:, cols] = mixed.astype(o_ref.dtype)


def _pool(proj, w_pool, pool_scale, batch, seq):
    n = proj.shape[0]
    ng, gw, _ = w_pool.shape
    width = ng * gw
    return pl.pallas_call(
        functools.partial(_pool_kernel, seq=seq, gw=gw),
        grid=(batch,),
        in_specs=[
            pl.BlockSpec((seq, width), lambda b: (b, 0)),
            _const_spec(w_pool.shape),
            _const_spec((1, width)),
        ],
        out_specs=pl.BlockSpec((seq, width), lambda b: (b, 0)),
        out_shape=jax.ShapeDtypeStruct((n, width), BF16),
        compiler_params=_cparams("parallel"),
        name="pool",
    )(proj, w_pool, pool_scale)


def _merge_kernel(og_ref, op_ref, pg_ref, x_ref, wb_ref, wo_ref, gn_ref, x1_ref, h_ref):
    dv = og_ref.shape[1]
    d = x_ref.shape[1]
    g0 = pg_ref.shape[1] - 2 * d
    yg = _dot(og_ref[...], wb_ref[0:dv, :])
    yp = _dot(op_ref[...], wb_ref[dv:, :])
    merged = (_sigmoid(pg_ref[:, g0:g0 + d].astype(F32)) * yg
              + _sigmoid(pg_ref[:, g0 + d:].astype(F32)) * yp)
    x1 = x_ref[...] + _dot(merged.astype(BF16), wo_ref[...])
    x1_ref[...] = x1
    h_ref[...] = _rms(x1, gn_ref[...]).astype(h_ref.dtype)


def _merge(o_gla, o_pool, proj_pg, x2d, w_branch, w_out, g_next, tm=256):
    n, d = x2d.shape
    dv = o_gla.shape[1]
    dp = o_pool.shape[1]
    wpg = proj_pg.shape[1]
    return pl.pallas_call(
        _merge_kernel,
        grid=(n // tm,),
        in_specs=[
            pl.BlockSpec((tm, dv), lambda i: (i, 0)),
            pl.BlockSpec((tm, dp), lambda i: (i, 0)),
            pl.BlockSpec((tm, wpg), lambda i: (i, 0)),
            pl.BlockSpec((tm, d), lambda i: (i, 0)),
            _const_spec(w_branch.shape),
            _const_spec(w_out.shape),
            _const_spec((1, d)),
        ],
        out_specs=[
            pl.BlockSpec((tm, d), lambda i: (i, 0)),
            pl.BlockSpec((tm, d), lambda i: (i, 0)),
        ],
        out_shape=[
            jax.ShapeDtypeStruct((n, d), F32),
            jax.ShapeDtypeStruct((n, d), BF16),
        ],
        compiler_params=_cparams("parallel"),
        name="merge",
    )(o_gla, o_pool, proj_pg, x2d, w_branch, w_out, g_next)


def _norm_mm_kernel(x_ref, g_ref, w_ref, o_ref, wbf_ref):
    @pl.when(pl.program_id(1) == 0)
    def _():
        wbf_ref[...] = w_ref[...].astype(BF16)

    h = _rms(x_ref[...], g_ref[...]).astype(BF16)
    o_ref[...] = _dot(h, wbf_ref[...]).astype(o_ref.dtype)


def _norm_matmul(x2d, g, w, tm, tn, name):
    n, d = x2d.shape
    m = w.shape[1]
    return pl.pallas_call(
        _norm_mm_kernel,
        grid=(m // tn, n // tm),
        in_specs=[
            pl.BlockSpec((tm, d), lambda j, i: (i, 0)),
            _const_spec((1, d)),
            pl.BlockSpec((d, tn), lambda j, i: (0, j)),
        ],
        out_specs=pl.BlockSpec((tm, tn), lambda j, i: (i, j)),
        out_shape=jax.ShapeDtypeStruct((n, m), BF16),
        scratch_shapes=[pltpu.VMEM((d, tn), BF16)],
        compiler_params=_cparams("parallel", "arbitrary"),
        name=name,
    )(x2d, g, w)


def _cross_kernel(h_ref, kv_ref, x_ref, wq_ref, wo_ref, gn_ref, x2_ref, hn_ref, perm_ref):
    tm, d = h_ref.shape
    assert tm == FFN_BLOCK
    hd = d // CROSS_HEADS
    q = _dot(h_ref[...], wq_ref[...]).astype(BF16)
    heads = []
    for hh in range(CROSS_HEADS):
        cols = slice(hh * hd, (hh + 1) * hd)
        s = _dot_nt(q[:, cols], kv_ref[:, cols]) * (hd ** -0.5)
        p = jnp.exp(s - jnp.max(s, axis=-1, keepdims=True))
        p = p * (1.0 / jnp.sum(p, axis=-1, keepdims=True))
        heads.append(_dot(p.astype(BF16), kv_ref[:, d + hh * hd:d + (hh + 1) * hd]).astype(BF16))
    o = jnp.concatenate(heads, axis=1)
    x2 = x_ref[...] + _dot(o, wo_ref[...])
    x2_ref[...] = x2
    hn = _rms(x2, gn_ref[...])
    for k in range(FFN_GROUPS):
        t0 = k * SUBLANES
        first = SUBLANES * (t0 % FFN_GROUPS) + t0 // FFN_GROUPS
        for c in range(perm_ref.shape[0]):
            perm_ref[c, pl.ds(first, SUBLANES, stride=SUBLANES), :] = (
                hn[t0:t0 + SUBLANES, c * LANES:(c + 1) * LANES])
    hn_ref[...] = jnp.concatenate([perm_ref[c] for c in range(perm_ref.shape[0])],
                                  axis=1).astype(hn_ref.dtype)


def _cross(h, kv, x2d, w_cq, w_co, g_next, batch, seq, n_mem, tm=256):
    n, d = x2d.shape
    spb = seq // tm
    return pl.pallas_call(
        _cross_kernel,
        grid=(batch, spb),
        in_specs=[
            pl.BlockSpec((tm, d), lambda b, i: (b * spb + i, 0)),
            pl.BlockSpec((n_mem, 2 * d), lambda b, i: (b, 0)),
            pl.BlockSpec((tm, d), lambda b, i: (b * spb + i, 0)),
            _const_spec(w_cq.shape),
            _const_spec(w_co.shape),
            _const_spec((1, d)),
        ],
        out_specs=[
            pl.BlockSpec((tm, d), lambda b, i: (b * spb + i, 0)),
            pl.BlockSpec((tm, d), lambda b, i: (b * spb + i, 0)),
        ],
        out_shape=[
            jax.ShapeDtypeStruct((n, d), F32),
            jax.ShapeDtypeStruct((n, d), BF16),
        ],
        scratch_shapes=[pltpu.VMEM((d // LANES, tm, LANES), F32)],
        compiler_params=_cparams("parallel", "parallel"),
        name="cross",
    )(h, kv, x2d, w_cq, w_co, g_next)


def _ffn_up_kernel(h_ref, wg32_ref, wv32_ref, cwg_ref, cwv_ref, cbg_ref, cbv_ref, act_ref,
                   wg_ref, wv_ref, *, rb):
    seq, tn = act_ref.shape
    row = lax.broadcasted_iota(jnp.int32, (SUBLANES, tn), 0)

    @pl.when(pl.program_id(1) == 0)
    def _():
        wg_ref[...] = wg32_ref[...].astype(BF16)
        wv_ref[...] = wv32_ref[...].astype(BF16)

    def wrap(group, prev_group):
        return jnp.where(row == 0, prev_group[SUBLANES - 1:SUBLANES], pltpu.roll(group, 1, 0))

    def conv(u, prev, cw_ref, cb_ref):
        w30 = wrap(u[rb - 2 * SUBLANES:rb - SUBLANES], prev[:SUBLANES])
        w31 = wrap(u[rb - SUBLANES:], prev[SUBLANES:])
        u1 = jnp.concatenate([w31, u[:rb - SUBLANES]], axis=0)
        u2 = jnp.concatenate([w30, w31, u[:rb - 2 * SUBLANES]], axis=0)
        return cw_ref[2:3] * u + cw_ref[1:2] * u1 + cw_ref[0:1] * u2 + cb_ref[...]

    prev_g = jnp.zeros((2 * SUBLANES, tn), F32)
    prev_v = jnp.zeros((2 * SUBLANES, tn), F32)
    for r0 in range(0, seq, rb):
        hb = h_ref[r0:r0 + rb, :]
        ug = _dot(hb, wg_ref[...])
        uv = _dot(hb, wv_ref[...])
        gate = conv(ug, prev_g, cwg_ref, cbg_ref)
        val = conv(uv, prev_v, cwv_ref, cbv_ref)
        prev_g = ug[rb - 2 * SUBLANES:]
        prev_v = uv[rb - 2 * SUBLANES:]
        act_ref[r0:r0 + rb, :] = (gate * _sigmoid(gate) * val).astype(act_ref.dtype)


def _ffn_up(h, w_up, conv_w, conv_b, batch, seq, tn=512, rb=FFN_BLOCK):
    n, d = h.shape
    dff = w_up.shape[1] // 2
    nj = dff // tn
    return pl.pallas_call(
        functools.partial(_ffn_up_kernel, rb=rb),
        grid=(nj, batch),
        in_specs=[
            pl.BlockSpec((seq, d), lambda j, b: (b, 0)),
            pl.BlockSpec((d, tn), lambda j, b: (0, j)),
            pl.BlockSpec((d, tn), lambda j, b: (0, nj + j)),
            pl.BlockSpec((CONV_W, tn), lambda j, b: (0, j)),
            pl.BlockSpec((CONV_W, tn), lambda j, b: (0, nj + j)),
            pl.BlockSpec((1, tn), lambda j, b: (0, j)),
            pl.BlockSpec((1, tn), lambda j, b: (0, nj + j)),
        ],
        out_specs=pl.BlockSpec((seq, tn), lambda j, b: (b, j)),
        out_shape=jax.ShapeDtypeStruct((n, dff), BF16),
        scratch_shapes=[pltpu.VMEM((d, tn), BF16), pltpu.VMEM((d, tn), BF16)],
        compiler_params=_cparams("parallel", "arbitrary"),
        name="ffn_up",
    )(h, w_up, w_up, conv_w, conv_w, conv_b, conv_b)


def _ffn_down_kernel(a_ref, w_ref, x_ref, g_ref, o_ref, y_ref):
    assert a_ref.shape[0] == FFN_BLOCK
    _store_lane_chunks(y_ref, _dot(a_ref[...], w_ref[...]))
    for k in range(FFN_GROUPS):
        t0 = k * SUBLANES
        first = SUBLANES * (t0 % FFN_GROUPS) + t0 // FFN_GROUPS
        y = _load_strided_rows(y_ref, first, SUBLANES)
        rows = slice(t0, t0 + SUBLANES)
        o_ref[rows, :] = _rms(x_ref[rows, :] + y, g_ref[...])


def _ffn_down(act, w_down, x2d, g, tm=FFN_BLOCK):
    n, d = x2d.shape
    dff = act.shape[1]
    return pl.pallas_call(
        _ffn_down_kernel,
        grid=(n // tm,),
        in_specs=[
            pl.BlockSpec((tm, dff), lambda i: (i, 0)),
            _const_spec(w_down.shape),
            pl.BlockSpec((tm, d), lambda i: (i, 0)),
            _const_spec((1, d)),
        ],
        out_specs=pl.BlockSpec((tm, d), lambda i: (i, 0)),
        out_shape=jax.ShapeDtypeStruct((n, d), F32),
        scratch_shapes=[pltpu.VMEM((d // LANES, tm, LANES), F32)],
        compiler_params=_cparams("parallel"),
        name="ffn_down",
    )(act, w_down, x2d, g)


def kernel(x, mem, g_mix, w_in, w_a2, b_a, g_gla, w_pool, pool_scale, w_branch, w_out,
           g_cross, g_mem, w_cq, w_ckv, w_co, g_ffn, w_up, conv_w, conv_b, w_down, g_final):
    batch, seq, d = x.shape
    n_mem = mem.shape[1]
    depth = w_in.shape[0]
    assert depth == 1
    dk = w_a2.shape[2]
    dv = g_gla.shape[1]
    off_a = 2 * dk + 2 * dv
    off_p = off_a + GLA_GATE_RANK
    row2 = lambda v: v.reshape(1, -1)

    xs = x.reshape(batch * seq, d)
    mem2d = mem.reshape(batch * n_mem, d)

    w_in_t = jnp.swapaxes(w_in[0], 0, 1)
    d_in = w_in_t.shape[0]
    wa2 = jnp.pad(w_a2[0], ((0, LANES - GLA_GATE_RANK), (0, 0))).astype(BF16)

    h, log_a = _norm_gate(xs, row2(g_mix[0]), w_in_t, off_a, wa2, row2(b_a[0]))
    proj_qkvr = _matmul_nt(h, w_in_t, 0, off_a, 1024, 1024, BF16, "in_proj_qkvr")
    proj_pg = _matmul_nt(h, w_in_t, off_p, d_in - off_p, 1024, 1024, BF16, "in_proj_pg")
    o_gla = _gla(proj_qkvr, log_a, row2(g_gla[0]), batch, seq, dk, dv)
    o_pool = _pool(proj_pg, w_pool[0].astype(BF16), row2(pool_scale[0]), batch, seq)
    x1, h2 = _merge(o_gla, o_pool, proj_pg, xs, w_branch[0].astype(BF16), w_out[0].astype(BF16),
                    row2(g_cross[0]))
    kv = _norm_matmul(mem2d, row2(g_mem[0]), w_ckv[0], 1024, 1024, "mem_kv")
    x2, h3 = _cross(h2, kv, x1, w_cq[0].astype(BF16), w_co[0].astype(BF16), row2(g_ffn[0]),
                    batch, seq, n_mem)
    act = _ffn_up(h3, w_up[0], conv_w[0], row2(conv_b[0]), batch, seq)
    out = _ffn_down(act, w_down[0].astype(BF16), x2, row2(g_final))
    return out.reshape(batch, seq, d)
```

```python
import functools

import numpy as np
import jax
import jax.numpy as jnp
from jax import lax
from jax.experimental import pallas as pl
from jax.experimental.pallas import tpu as pltpu

F32 = jnp.float32
BF16 = jnp.bfloat16

EPS = 1e-6
LOG2_E = 1.4426950408889634
GLA_HEADS = 4
GLA_GATE_RANK = 16
GLA_GATE_NORM = 16.0
GLA_CHUNK = 64
POOL_WINDOWS = (2, 4, 8, 16)
CROSS_HEADS = 4
CONV_W = 3

LANES = 128
SUBLANES = 8
VMEM_LIMIT = 52 * 1024 * 1024

FFN_BLOCK = 256
FFN_GROUPS = FFN_BLOCK // SUBLANES

GLA_LEVELS = (32, 16, 8, 4, 2, 1)
GLA_SPLIT = 3


def _cparams(*sem):
    return pltpu.CompilerParams(dimension_semantics=sem, vmem_limit_bytes=VMEM_LIMIT)


def _const_spec(shape):
    nd = len(shape)
    return pl.BlockSpec(shape, lambda *_: (0,) * nd, pipeline_mode=pl.Buffered(1))


def _rms(xf, g):
    return xf * lax.rsqrt(jnp.mean(xf * xf, axis=-1, keepdims=True) + EPS) * g


def _sigmoid(x):
    return 1.0 / (1.0 + jnp.exp(-x))


def _store_lane_chunks(ref, value):
    for c in range(ref.shape[0]):
        ref[c] = value[:, c * LANES:(c + 1) * LANES]


def _load_strided_rows(ref, first, stride):
    return jnp.concatenate([ref[c, pl.ds(first, SUBLANES, stride=stride), :]
                            for c in range(ref.shape[0])], axis=1)


def _dot(a, b):
    return jnp.dot(a, b, preferred_element_type=F32)


def _dot_nt(a, b):
    return lax.dot_general(a, b, (((1,), (1,)), ((), ())), preferred_element_type=F32)


def _dot_tn(a, b):
    return lax.dot_general(a, b, (((0,), (0,)), ((), ())), preferred_element_type=F32)


def _norm_gate_kernel(x_ref, g_ref, wa1_ref, wa2_ref, ba_ref, h_ref, la_ref):
    h = _rms(x_ref[...], g_ref[...]).astype(BF16)
    h_ref[...] = h
    a = _dot_nt(h, wa1_ref[...].astype(BF16))
    gp = _dot(a.astype(BF16), wa2_ref[...]) + ba_ref[...]
    ls = jnp.minimum(gp, 0.0) - jnp.log1p(jnp.exp(-jnp.abs(gp)))
    la_ref[...] = ls * (1.0 / GLA_GATE_NORM)


def _norm_gate(x2d, g, w_in_t, gate_row0, wa2, ba, tm=1024):
    n, d = x2d.shape
    dk = wa2.shape[1]
    return pl.pallas_call(
        _norm_gate_kernel,
        grid=(n // tm,),
        in_specs=[
            pl.BlockSpec((tm, d), lambda i: (i, 0)),
            _const_spec((1, d)),
            pl.BlockSpec((LANES, d), lambda i: (gate_row0 // LANES, 0), pipeline_mode=pl.Buffered(1)),
            _const_spec(wa2.shape),
            _const_spec((1, dk)),
        ],
        out_specs=[
            pl.BlockSpec((tm, d), lambda i: (i, 0)),
            pl.BlockSpec((tm, dk), lambda i: (i, 0)),
        ],
        out_shape=[
            jax.ShapeDtypeStruct((n, d), BF16),
            jax.ShapeDtypeStruct((n, dk), F32),
        ],
        compiler_params=_cparams("parallel"),
        name="norm_gate",
    )(x2d, g, w_in_t, wa2, ba)


def _mm_nt_kernel(a_ref, wt_ref, o_ref, wbf_ref):
    @pl.when(pl.program_id(1) == 0)
    def _():
        wbf_ref[...] = wt_ref[...].astype(BF16)

    o_ref[...] = _dot_nt(a_ref[...], wbf_ref[...]).astype(o_ref.dtype)


def _matmul_nt(a, wt, row0, m, tm, tn, out_dtype, name):
    n, k = a.shape
    return pl.pallas_call(
        _mm_nt_kernel,
        grid=(m // tn, n // tm),
        in_specs=[
            pl.BlockSpec((tm, k), lambda j, i: (i, 0)),
            pl.BlockSpec((pl.Element(tn), pl.Element(k)),
                         lambda j, i: (pl.multiple_of(row0 + j * tn, SUBLANES), 0)),
        ],
        out_specs=pl.BlockSpec((tm, tn), lambda j, i: (i, j)),
        out_shape=jax.ShapeDtypeStruct((n, m), out_dtype),
        scratch_shapes=[pltpu.VMEM((tn, k), BF16)],
        compiler_params=_cparams("parallel", "arbitrary"),
        name=name,
    )(a, wt)


def _gla_consts():
    c = GLA_CHUNK
    i = np.arange(c)[:, None]
    t = np.arange(c)[None, :]
    blocks = [t <= i, t > i]
    masks = []
    for s in GLA_LEVELS:
        upper = (i % (2 * s)) >= s
        blk_start = i - (i % s)
        blk_end = blk_start + s - 1
        blocks.append((upper & (t >= blk_start) & (t <= i)) | (~upper & (t > i) & (t <= blk_end)))
        same_group = (i // (2 * s)) == (t // (2 * s))
        masks.append(same_group & upper & ((t % (2 * s)) < s))
    masks.append(i == t)
    assert (np.sum(masks, axis=0) == (t <= i)).all()
    expo = np.concatenate(blocks, axis=0).astype(np.float32)
    expo = np.concatenate([expo] * GLA_SPLIT, axis=1)
    return jnp.asarray(expo, BF16), jnp.asarray(np.stack(masks).astype(np.float32))


def _gla_kernel(q_ref, k_ref, v_ref, r_ref, la_ref, gg_ref, expo_ref, mask_ref, o_ref, st_ref,
                *, q_scale):
    c = GLA_CHUNK
    n_levels = len(GLA_LEVELS)
    hk = q_ref.shape[1] // GLA_HEADS
    hv = v_ref.shape[1] // GLA_HEADS

    @pl.when(pl.program_id(1) == 0)
    def _():
        st_ref[...] = jnp.zeros_like(st_ref)

    def chunk(ci, carry):
        r0 = pl.multiple_of(ci * c, c)
        rows = pl.ds(r0, c)
        g = la_ref[rows, :] * LOG2_E
        g1 = g.astype(BF16)
        rem = g - g1.astype(F32)
        g2 = rem.astype(BF16)
        g3 = (rem - g2.astype(F32)).astype(BF16)
        dec_all = jnp.exp2(_dot(expo_ref[...], jnp.concatenate([g1, g2, g3], axis=0)))
        for hh in range(GLA_HEADS):
            kc = slice(hh * hk, (hh + 1) * hk)
            vc = slice(hh * hv, (hh + 1) * hv)
            dec = dec_all[:, kc]

            q = q_ref[rows, kc].astype(F32) * q_scale
            k = k_ref[rows, kc].astype(F32)
            v = v_ref[rows, vc]

            att = mask_ref[n_levels] * _dot_nt(q.astype(BF16), k.astype(BF16))
            for l in range(n_levels):
                f = dec[(2 + l) * c:(3 + l) * c]
                att = att + mask_ref[l] * _dot_nt((q * f).astype(BF16), (k * f).astype(BF16))

            q_in = (q * dec[0:c]).astype(BF16)
            k_out = (k * dec[c:2 * c]).astype(BF16)
            st = st_ref[hh]
            o = _dot(att.astype(BF16), v) + _dot_nt(q_in, st.astype(BF16))
            st_ref[hh] = st * dec[c - 1:c] + _dot_tn(v, k_out)

            on = o * lax.rsqrt(jnp.mean(o * o, axis=-1, keepdims=True) + EPS) * gg_ref[:, vc]
            r = r_ref[rows, vc].astype(F32)
            o_ref[rows, vc] = (on * (r * _sigmoid(r))).astype(o_ref.dtype)
        return carry

    lax.fori_loop(0, q_ref.shape[0] // c, chunk, 0, unroll=8)


def _gla(proj, log_a, g_gla, batch, seq, dk, dv, tt=1024):
    n = proj.shape[0]
    hk = dk // GLA_HEADS
    hv = dv // GLA_HEADS
    expo, masks = _gla_consts()
    spb = seq // tt
    v_blk = 2 * dk // dv
    return pl.pallas_call(
        functools.partial(_gla_kernel, q_scale=hk ** -0.5),
        grid=(batch, spb),
        in_specs=[
            pl.BlockSpec((tt, dk), lambda b, s: (b * spb + s, 0)),
            pl.BlockSpec((tt, dk), lambda b, s: (b * spb + s, 1)),
            pl.BlockSpec((tt, dv), lambda b, s: (b * spb + s, v_blk)),
            pl.BlockSpec((tt, dv), lambda b, s: (b * spb + s, v_blk + 1)),
            pl.BlockSpec((tt, dk), lambda b, s: (b * spb + s, 0)),
            _const_spec((1, dv)),
            _const_spec(expo.shape),
            _const_spec(masks.shape),
        ],
        out_specs=pl.BlockSpec((tt, dv), lambda b, s: (b * spb + s, 0)),
        out_shape=jax.ShapeDtypeStruct((n, dv), BF16),
        scratch_shapes=[pltpu.VMEM((GLA_HEADS, hv, hk), F32)],
        compiler_params=_cparams("parallel", "arbitrary"),
        name="gla",
    )(proj, proj, proj, proj, log_a, g_gla, expo, masks)


def _pool_kernel(p_ref, w_ref, sc_ref, o_ref, *, seq, gw):
    row = lax.broadcasted_iota(jnp.int32, (seq, gw), 0)
    row1 = lax.broadcasted_iota(jnp.int32, (seq, 1), 0)
    for gi, w in enumerate(POOL_WINDOWS):
        cols = slice(gi * gw, (gi + 1) * gw)
        x = p_ref[:, cols].astype(F32)
        acc = x
        d = 1
        while d < w:
            acc = acc + jnp.where(row >= d, pltpu.roll(acc, d, 0), 0.0)
            d *= 2
        inv_cnt = 1.0 / jnp.minimum(row1 + 1, w).astype(F32)
        pooled = acc * inv_cnt - x
        mixed = _dot(pooled.astype(BF16), w_ref[gi]) * sc_ref[:, cols]
        o_ref[:, cols] = mixed.astype(o_ref.dtype)


def _pool(proj, w_pool, pool_scale, batch, seq):
    n = proj.shape[0]
    ng, gw, _ = w_pool.shape
    width = ng * gw
    return pl.pallas_call(
        functools.partial(_pool_kernel, seq=seq, gw=gw),
        grid=(batch,),
        in_specs=[
            pl.BlockSpec((seq, width), lambda b: (b, 0)),
            _const_spec(w_pool.shape),
            _const_spec((1, width)),
        ],
        out_specs=pl.BlockSpec((seq, width), lambda b: (b, 0)),
        out_shape=jax.ShapeDtypeStruct((n, width), BF16),
        compiler_params=_cparams("parallel"),
        name="pool",
    )(proj, w_pool, pool_scale)


def _merge_kernel(og_ref, op_ref, pg_ref, x_ref, wb_ref, wo_ref, gn_ref, x1_ref, h_ref):
    dv = og_ref.shape[1]
    d = x_ref.shape[1]
    g0 = pg_ref.shape[1] - 2 * d
    yg = _dot(og_ref[...], wb_ref[0:dv, :])
    yp = _dot(op_ref[...], wb_ref[dv:, :])
    merged = (_sigmoid(pg_ref[:, g0:g0 + d].astype(F32)) * yg
              + _sigmoid(pg_ref[:, g0 + d:].astype(F32)) * yp)
    x1 = x_ref[...] + _dot(merged.astype(BF16), wo_ref[...])
    x1_ref[...] = x1
    h_ref[...] = _rms(x1, gn_ref[...]).astype(h_ref.dtype)


def _merge(o_gla, o_pool, proj_pg, x2d, w_branch, w_out, g_next, tm=256):
    n, d = x2d.shape
    dv = o_gla.shape[1]
    dp = o_pool.shape[1]
    wpg = proj_pg.shape[1]
    return pl.pallas_call(
        _merge_kernel,
        grid=(n // tm,),
        in_specs=[
            pl.BlockSpec((tm, dv), lambda i: (i, 0)),
            pl.BlockSpec((tm, dp), lambda i: (i, 0)),
            pl.BlockSpec((tm, wpg), lambda i: (i, 0)),
            pl.BlockSpec((tm, d), lambda i: (i, 0)),
            _const_spec(w_branch.shape),
            _const_spec(w_out.shape),
            _const_spec((1, d)),
        ],
        out_specs=[
            pl.BlockSpec((tm, d), lambda i: (i, 0)),
            pl.BlockSpec((tm, d), lambda i: (i, 0)),
        ],
        out_shape=[
            jax.ShapeDtypeStruct((n, d), F32),
            jax.ShapeDtypeStruct((n, d), BF16),
        ],
        compiler_params=_cparams("parallel"),
        name="merge",
    )(o_gla, o_pool, proj_pg, x2d, w_branch, w_out, g_next)


def _norm_mm_kernel(x_ref, g_ref, w_ref, o_ref, wbf_ref):
    @pl.when(pl.program_id(1) == 0)
    def _():
        wbf_ref[...] = w_ref[...].astype(BF16)

    h = _rms(x_ref[...], g_ref[...]).astype(BF16)
    o_ref[...] = _dot(h, wbf_ref[...]).astype(o_ref.dtype)


def _norm_matmul(x2d, g, w, tm, tn, name):
    n, d = x2d.shape
    m = w.shape[1]
    return pl.pallas_call(
        _norm_mm_kernel,
        grid=(m // tn, n // tm),
        in_specs=[
            pl.BlockSpec((tm, d), lambda j, i: (i, 0)),
            _const_spec((1, d)),
            pl.BlockSpec((d, tn), lambda j, i: (0, j)),
        ],
        out_specs=pl.BlockSpec((tm, tn), lambda j, i: (i, j)),
        out_shape=jax.ShapeDtypeStruct((n, m), BF16),
        scratch_shapes=[pltpu.VMEM((d, tn), BF16)],
        compiler_params=_cparams("parallel", "arbitrary"),
        name=name,
    )(x2d, g, w)


def _cross_fold_kernel(kv_ref, wq_ref, wo_ref, wqk_ref, vw_ref):
    d = wq_ref.shape[0]
    n_mem = kv_ref.shape[0]
    hd = d // CROSS_HEADS
    for hh in range(CROSS_HEADS):
        cols = slice(hh * hd, (hh + 1) * hd)
        wqk = _dot_nt(wq_ref[:, cols], kv_ref[:, cols]) * (hd ** -0.5)
        wqk_ref[:, hh * n_mem:(hh + 1) * n_mem] = wqk.astype(wqk_ref.dtype)
        vw = _dot(kv_ref[:, d + hh * hd:d + (hh + 1) * hd], wo_ref[cols, :])
        vw_ref[hh * n_mem:(hh + 1) * n_mem, :] = vw.astype(vw_ref.dtype)


def _cross_fold(kv, w_cq, w_co, batch, n_mem):
    d = w_cq.shape[0]
    hm = CROSS_HEADS * n_mem
    return pl.pallas_call(
        _cross_fold_kernel,
        grid=(batch,),
        in_specs=[
            pl.BlockSpec((n_mem, 2 * d), lambda b: (b, 0)),
            _const_spec(w_cq.shape),
            _const_spec(w_co.shape),
        ],
        out_specs=[
            pl.BlockSpec((d, hm), lambda b: (b, 0)),
            pl.BlockSpec((hm, d), lambda b: (b, 0)),
        ],
        out_shape=[
            jax.ShapeDtypeStruct((batch * d, hm), BF16),
            jax.ShapeDtypeStruct((batch * hm, d), BF16),
        ],
        compiler_params=_cparams("parallel"),
        name="cross_fold",
    )(kv, w_cq, w_co)


def _cross_kernel(h_ref, wqk_ref, vw_ref, x_ref, gn_ref, x2_ref, hn_ref, perm_ref):
    tm, d = h_ref.shape
    assert tm == FFN_BLOCK
    n_mem = wqk_ref.shape[1] // CROSS_HEADS
    s_all = _dot(h_ref[...], wqk_ref[...])
    probs = []
    for hh in range(CROSS_HEADS):
        s = s_all[:, hh * n_mem:(hh + 1) * n_mem]
        p = jnp.exp(s - jnp.max(s, axis=-1, keepdims=True))
        probs.append((p * (1.0 / jnp.sum(p, axis=-1, keepdims=True))).astype(BF16))
    x2 = x_ref[...] + _dot(jnp.concatenate(probs, axis=1), vw_ref[...])
    x2_ref[...] = x2
    hn = _rms(x2, gn_ref[...])
    for k in range(FFN_GROUPS):
        t0 = k * SUBLANES
        first = SUBLANES * (t0 % FFN_GROUPS) + t0 // FFN_GROUPS
        for c in range(perm_ref.shape[0]):
            perm_ref[c, pl.ds(first, SUBLANES, stride=SUBLANES), :] = (
                hn[t0:t0 + SUBLANES, c * LANES:(c + 1) * LANES])
    hn_ref[...] = jnp.concatenate([perm_ref[c] for c in range(perm_ref.shape[0])],
                                  axis=1).astype(hn_ref.dtype)


def _cross(h, wqk, vw, x2d, g_next, batch, seq, tm=FFN_BLOCK):
    n, d = x2d.shape
    hm = wqk.shape[1]
    spb = seq // tm
    return pl.pallas_call(
        _cross_kernel,
        grid=(batch, spb),
        in_specs=[
            pl.BlockSpec((tm, d), lambda b, i: (b * spb + i, 0)),
            pl.BlockSpec((d, hm), lambda b, i: (b, 0)),
            pl.BlockSpec((hm, d), lambda b, i: (b, 0)),
            pl.BlockSpec((tm, d), lambda b, i: (b * spb + i, 0)),
            _const_spec((1, d)),
        ],
        out_specs=[
            pl.BlockSpec((tm, d), lambda b, i: (b * spb + i, 0)),
            pl.BlockSpec((tm, d), lambda b, i: (b * spb + i, 0)),
        ],
        out_shape=[
            jax.ShapeDtypeStruct((n, d), F32),
            jax.ShapeDtypeStruct((n, d), BF16),
        ],
        scratch_shapes=[pltpu.VMEM((d // LANES, tm, LANES), F32)],
        compiler_params=_cparams("parallel", "parallel"),
        name="cross",
    )(h, wqk, vw, x2d, g_next)


def _ffn_up_kernel(h_ref, wg32_ref, wv32_ref, cwg_ref, cwv_ref, cbg_ref, cbv_ref, act_ref,
                   wg_ref, wv_ref, *, rb):
    seq, tn = act_ref.shape
    row = lax.broadcasted_iota(jnp.int32, (SUBLANES, tn), 0)

    @pl.when(pl.program_id(1) == 0)
    def _():
        wg_ref[...] = wg32_ref[...].astype(BF16)
        wv_ref[...] = wv32_ref[...].astype(BF16)

    def wrap(group, prev_group):
        return jnp.where(row == 0, prev_group[SUBLANES - 1:SUBLANES], pltpu.roll(group, 1, 0))

    def conv(u, prev, cw_ref, cb_ref):
        w30 = wrap(u[rb - 2 * SUBLANES:rb - SUBLANES], prev[:SUBLANES])
        w31 = wrap(u[rb - SUBLANES:], prev[SUBLANES:])
        u1 = jnp.concatenate([w31, u[:rb - SUBLANES]], axis=0)
        u2 = jnp.concatenate([w30, w31, u[:rb - 2 * SUBLANES]], axis=0)
        return cw_ref[2:3] * u + cw_ref[1:2] * u1 + cw_ref[0:1] * u2 + cb_ref[...]

    prev_g = jnp.zeros((2 * SUBLANES, tn), F32)
    prev_v = jnp.zeros((2 * SUBLANES, tn), F32)
    for r0 in range(0, seq, rb):
        hb = h_ref[r0:r0 + rb, :]
        ug = _dot(hb, wg_ref[...])
        uv = _dot(hb, wv_ref[...])
        gate = conv(ug, prev_g, cwg_ref, cbg_ref)
        val = conv(uv, prev_v, cwv_ref, cbv_ref)
        prev_g = ug[rb - 2 * SUBLANES:]
        prev_v = uv[rb - 2 * SUBLANES:]
        act_ref[r0:r0 + rb, :] = (gate * _sigmoid(gate) * val).astype(act_ref.dtype)


def _ffn_up(h, w_up, conv_w, conv_b, batch, seq, tn=512, rb=FFN_BLOCK):
    n, d = h.shape
    dff = w_up.shape[1] // 2
    nj = dff // tn
    return pl.pallas_call(
        functools.partial(_ffn_up_kernel, rb=rb),
        grid=(nj, batch),
        in_specs=[
            pl.BlockSpec((seq, d), lambda j, b: (b, 0)),
            pl.BlockSpec((d, tn), lambda j, b: (0, j)),
            pl.BlockSpec((d, tn), lambda j, b: (0, nj + j)),
            pl.BlockSpec((CONV_W, tn), lambda j, b: (0, j)),
            pl.BlockSpec((CONV_W, tn), lambda j, b: (0, nj + j)),
            pl.BlockSpec((1, tn), lambda j, b: (0, j)),
            pl.BlockSpec((1, tn), lambda j, b: (0, nj + j)),
        ],
        out_specs=pl.BlockSpec((seq, tn), lambda j, b: (b, j)),
        out_shape=jax.ShapeDtypeStruct((n, dff), BF16),
        scratch_shapes=[pltpu.VMEM((d, tn), BF16), pltpu.VMEM((d, tn), BF16)],
        compiler_params=_cparams("parallel", "arbitrary"),
        name="ffn_up",
    )(h, w_up, w_up, conv_w, conv_w, conv_b, conv_b)


def _ffn_down_kernel(a_ref, w_ref, x_ref, g_ref, o_ref, y_ref):
    assert a_ref.shape[0] == FFN_BLOCK
    _store_lane_chunks(y_ref, _dot(a_ref[...], w_ref[...]))
    for k in range(FFN_GROUPS):
        t0 = k * SUBLANES
        first = SUBLANES * (t0 % FFN_GROUPS) + t0 // FFN_GROUPS
        y = _load_strided_rows(y_ref, first, SUBLANES)
        rows = slice(t0, t0 + SUBLANES)
        o_ref[rows, :] = _rms(x_ref[rows, :] + y, g_ref[...])


def _ffn_down(act, w_down, x2d, g, tm=FFN_BLOCK):
    n, d = x2d.shape
    dff = act.shape[1]
    return pl.pallas_call(
        _ffn_down_kernel,
        grid=(n // tm,),
        in_specs=[
            pl.BlockSpec((tm, dff), lambda i: (i, 0)),
            _const_spec(w_down.shape),
            pl.BlockSpec((tm, d), lambda i: (i, 0)),
            _const_spec((1, d)),
        ],
        out_specs=pl.BlockSpec((tm, d), lambda i: (i, 0)),
        out_shape=jax.ShapeDtypeStruct((n, d), F32),
        scratch_shapes=[pltpu.VMEM((d // LANES, tm, LANES), F32)],
        compiler_params=_cparams("parallel"),
        name="ffn_down",
    )(act, w_down, x2d, g)


def kernel(x, mem, g_mix, w_in, w_a2, b_a, g_gla, w_pool, pool_scale, w_branch, w_out,
           g_cross, g_mem, w_cq, w_ckv, w_co, g_ffn, w_up, conv_w, conv_b, w_down, g_final):
    batch, seq, d = x.shape
    n_mem = mem.shape[1]
    depth = w_in.shape[0]
    assert depth == 1
    dk = w_a2.shape[2]
    dv = g_gla.shape[1]
    off_a = 2 * dk + 2 * dv
    off_p = off_a + GLA_GATE_RANK
    row2 = lambda v: v.reshape(1, -1)

    xs = x.reshape(batch * seq, d)
    mem2d = mem.reshape(batch * n_mem, d)

    w_in_t = jnp.swapaxes(w_in[0], 0, 1)
    d_in = w_in_t.shape[0]
    wa2 = jnp.pad(w_a2[0], ((0, LANES - GLA_GATE_RANK), (0, 0))).astype(BF16)

    h, log_a = _norm_gate(xs, row2(g_mix[0]), w_in_t, off_a, wa2, row2(b_a[0]))
    proj_qkvr = _matmul_nt(h, w_in_t, 0, off_a, 1024, 1024, BF16, "in_proj_qkvr")
    proj_pg = _matmul_nt(h, w_in_t, off_p, d_in - off_p, 1024, 1024, BF16, "in_proj_pg")
    o_gla = _gla(proj_qkvr, log_a, row2(g_gla[0]), batch, seq, dk, dv)
    o_pool = _pool(proj_pg, w_pool[0].astype(BF16), row2(pool_scale[0]), batch, seq)
    x1, h2 = _merge(o_gla, o_pool, proj_pg, xs, w_branch[0].astype(BF16), w_out[0].astype(BF16),
                    row2(g_cross[0]))
    kv = _norm_matmul(mem2d, row2(g_mem[0]), w_ckv[0], 1024, 1024, "mem_kv")
    wqk, vw = _cross_fold(kv, w_cq[0].astype(BF16), w_co[0].astype(BF16), batch, n_mem)
    x2, h3 = _cross(h2, wqk, vw, x1, row2(g_ffn[0]), batch, seq)
    act = _ffn_up(h3, w_up[0], conv_w[0], row2(conv_b[0]), batch, seq)
    out = _ffn_down(act, w_down[0].astype(BF16), x2, row2(g_final))
    return out.reshape(batch, seq, d)
```

```python
import functools

import numpy as np
import jax
import jax.numpy as jnp
from jax import lax
from jax.experimental import pallas as pl
from jax.experimental.pallas import tpu as pltpu

F32 = jnp.float32
BF16 = jnp.bfloat16

EPS = 1e-6
LOG2_E = 1.4426950408889634
GLA_HEADS = 4
GLA_GATE_RANK = 16
GLA_GATE_NORM = 16.0
GLA_CHUNK = 64
POOL_WINDOWS = (2, 4, 8, 16)
CROSS_HEADS = 4
CONV_W = 3

LANES = 128
SUBLANES = 8
VMEM_LIMIT = 52 * 1024 * 1024

FFN_BLOCK = 256
FFN_GROUPS = FFN_BLOCK // SUBLANES

GLA_LEVELS = (32, 16, 8, 4, 2, 1)
GLA_SPLIT = 3


def _cparams(*sem):
    return pltpu.CompilerParams(dimension_semantics=sem, vmem_limit_bytes=VMEM_LIMIT)


def _const_spec(shape):
    nd = len(shape)
    return pl.BlockSpec(shape, lambda *_: (0,) * nd, pipeline_mode=pl.Buffered(1))


def _rms(xf, g):
    return xf * lax.rsqrt(jnp.mean(xf * xf, axis=-1, keepdims=True) + EPS) * g


def _sigmoid(x):
    return 1.0 / (1.0 + jnp.exp(-x))


def _store_lane_chunks(ref, value):
    for c in range(ref.shape[0]):
        ref[c] = value[:, c * LANES:(c + 1) * LANES]


def _load_strided_rows(ref, first, stride):
    return jnp.concatenate([ref[c, pl.ds(first, SUBLANES, stride=stride), :]
                            for c in range(ref.shape[0])], axis=1)


STAGE_ROWS = 256


def _fetch_as_bf16(w_hbm, w_bf, stage, sem):
    rows = stage.shape[1]
    n_chunks = w_hbm.shape[0] // rows

    def copy(k):
        return pltpu.make_async_copy(w_hbm.at[pl.ds(k * rows, rows), :], stage.at[k % 2],
                                     sem.at[k % 2])

    copy(0).start()
    for k in range(n_chunks):
        if k + 1 < n_chunks:
            copy(k + 1).start()
        copy(k).wait()
        w_bf[k * rows:(k + 1) * rows, :] = stage[k % 2].astype(BF16)


def _resident_weight_scratch(*weights):
    cols = weights[0].shape[1]
    assert all(w.shape[1] == cols and w.shape[0] % STAGE_ROWS == 0 for w in weights)
    return ([pltpu.VMEM(w.shape, BF16) for w in weights]
            + [pltpu.VMEM((2, STAGE_ROWS, cols), F32), pltpu.SemaphoreType.DMA((2,))])


def _dot(a, b):
    return jnp.dot(a, b, preferred_element_type=F32)


def _dot_nt(a, b):
    return lax.dot_general(a, b, (((1,), (1,)), ((), ())), preferred_element_type=F32)


def _dot_tn(a, b):
    return lax.dot_general(a, b, (((0,), (0,)), ((), ())), preferred_element_type=F32)


def _norm_gate_kernel(x_ref, g_ref, wa1_ref, wa2_ref, ba_ref, h_ref, la_ref):
    h = _rms(x_ref[...], g_ref[...]).astype(BF16)
    h_ref[...] = h
    a = _dot_nt(h, wa1_ref[...].astype(BF16))
    gp = _dot(a.astype(BF16), wa2_ref[...]) + ba_ref[...]
    ls = jnp.minimum(gp, 0.0) - jnp.log1p(jnp.exp(-jnp.abs(gp)))
    la_ref[...] = ls * (1.0 / GLA_GATE_NORM)


def _norm_gate(x2d, g, w_in_t, gate_row0, wa2, ba, tm=1024):
    n, d = x2d.shape
    dk = wa2.shape[1]
    return pl.pallas_call(
        _norm_gate_kernel,
        grid=(n // tm,),
        in_specs=[
            pl.BlockSpec((tm, d), lambda i: (i, 0)),
            _const_spec((1, d)),
            pl.BlockSpec((LANES, d), lambda i: (gate_row0 // LANES, 0), pipeline_mode=pl.Buffered(1)),
            _const_spec(wa2.shape),
            _const_spec((1, dk)),
        ],
        out_specs=[
            pl.BlockSpec((tm, d), lambda i: (i, 0)),
            pl.BlockSpec((tm, dk), lambda i: (i, 0)),
        ],
        out_shape=[
            jax.ShapeDtypeStruct((n, d), BF16),
            jax.ShapeDtypeStruct((n, dk), F32),
        ],
        compiler_params=_cparams("parallel"),
        name="norm_gate",
    )(x2d, g, w_in_t, wa2, ba)


def _mm_nt_kernel(a_ref, wt_ref, o_ref, wbf_ref):
    @pl.when(pl.program_id(1) == 0)
    def _():
        wbf_ref[...] = wt_ref[...].astype(BF16)

    o_ref[...] = _dot_nt(a_ref[...], wbf_ref[...]).astype(o_ref.dtype)


def _matmul_nt(a, wt, row0, m, tm, tn, out_dtype, name):
    n, k = a.shape
    return pl.pallas_call(
        _mm_nt_kernel,
        grid=(m // tn, n // tm),
        in_specs=[
            pl.BlockSpec((tm, k), lambda j, i: (i, 0)),
            pl.BlockSpec((pl.Element(tn), pl.Element(k)),
                         lambda j, i: (pl.multiple_of(row0 + j * tn, SUBLANES), 0)),
        ],
        out_specs=pl.BlockSpec((tm, tn), lambda j, i: (i, j)),
        out_shape=jax.ShapeDtypeStruct((n, m), out_dtype),
        scratch_shapes=[pltpu.VMEM((tn, k), BF16)],
        compiler_params=_cparams("parallel", "arbitrary"),
        name=name,
    )(a, wt)


def _gla_consts():
    c = GLA_CHUNK
    i = np.arange(c)[:, None]
    t = np.arange(c)[None, :]
    blocks = [t <= i, t > i]
    masks = []
    for s in GLA_LEVELS:
        upper = (i % (2 * s)) >= s
        blk_start = i - (i % s)
        blk_end = blk_start + s - 1
        blocks.append((upper & (t >= blk_start) & (t <= i)) | (~upper & (t > i) & (t <= blk_end)))
        same_group = (i // (2 * s)) == (t // (2 * s))
        masks.append(same_group & upper & ((t % (2 * s)) < s))
    masks.append(i == t)
    assert (np.sum(masks, axis=0) == (t <= i)).all()
    expo = np.concatenate(blocks, axis=0).astype(np.float32)
    expo = np.concatenate([expo] * GLA_SPLIT, axis=1)
    return jnp.asarray(expo, BF16), jnp.asarray(np.stack(masks).astype(np.float32))


def _gla_kernel(q_ref, k_ref, v_ref, r_ref, la_ref, gg_ref, expo_ref, mask_ref, o_ref, st_ref,
                *, q_scale):
    c = GLA_CHUNK
    n_levels = len(GLA_LEVELS)
    hk = q_ref.shape[1] // GLA_HEADS
    hv = v_ref.shape[1] // GLA_HEADS

    @pl.when(pl.program_id(1) == 0)
    def _():
        st_ref[...] = jnp.zeros_like(st_ref)

    def chunk(ci, carry):
        r0 = pl.multiple_of(ci * c, c)
        rows = pl.ds(r0, c)
        g = la_ref[rows, :] * LOG2_E
        g1 = g.astype(BF16)
        rem = g - g1.astype(F32)
        g2 = rem.astype(BF16)
        g3 = (rem - g2.astype(F32)).astype(BF16)
        dec_all = jnp.exp2(_dot(expo_ref[...], jnp.concatenate([g1, g2, g3], axis=0)))
        for hh in range(GLA_HEADS):
            kc = slice(hh * hk, (hh + 1) * hk)
            vc = slice(hh * hv, (hh + 1) * hv)
            dec = dec_all[:, kc]

            q = q_ref[rows, kc].astype(F32) * q_scale
            k = k_ref[rows, kc].astype(F32)
            v = v_ref[rows, vc]

            att = mask_ref[n_levels] * _dot_nt(q.astype(BF16), k.astype(BF16))
            for l in range(n_levels):
                f = dec[(2 + l) * c:(3 + l) * c]
                att = att + mask_ref[l] * _dot_nt((q * f).astype(BF16), (k * f).astype(BF16))

            q_in = (q * dec[0:c]).astype(BF16)
            k_out = (k * dec[c:2 * c]).astype(BF16)
            st = st_ref[hh]
            o = _dot(att.astype(BF16), v) + _dot_nt(q_in, st.astype(BF16))
            st_ref[hh] = st * dec[c - 1:c] + _dot_tn(v, k_out)

            on = o * lax.rsqrt(jnp.mean(o * o, axis=-1, keepdims=True) + EPS) * gg_ref[:, vc]
            r = r_ref[rows, vc].astype(F32)
            o_ref[rows, vc] = (on * (r * _sigmoid(r))).astype(o_ref.dtype)
        return carry

    lax.fori_loop(0, q_ref.shape[0] // c, chunk, 0, unroll=8)


def _gla(proj, log_a, g_gla, batch, seq, dk, dv, tt=1024):
    n = proj.shape[0]
    hk = dk // GLA_HEADS
    hv = dv // GLA_HEADS
    expo, masks = _gla_consts()
    spb = seq // tt
    v_blk = 2 * dk // dv
    return pl.pallas_call(
        functools.partial(_gla_kernel, q_scale=hk ** -0.5),
        grid=(batch, spb),
        in_specs=[
            pl.BlockSpec((tt, dk), lambda b, s: (b * spb + s, 0)),
            pl.BlockSpec((tt, dk), lambda b, s: (b * spb + s, 1)),
            pl.BlockSpec((tt, dv), lambda b, s: (b * spb + s, v_blk)),
            pl.BlockSpec((tt, dv), lambda b, s: (b * spb + s, v_blk + 1)),
            pl.BlockSpec((tt, dk), lambda b, s: (b * spb + s, 0)),
            _const_spec((1, dv)),
            _const_spec(expo.shape),
            _const_spec(masks.shape),
        ],
        out_specs=pl.BlockSpec((tt, dv), lambda b, s: (b * spb + s, 0)),
        out_shape=jax.ShapeDtypeStruct((n, dv), BF16),
        scratch_shapes=[pltpu.VMEM((GLA_HEADS, hv, hk), F32)],
        compiler_params=_cparams("parallel", "arbitrary"),
        name="gla",
    )(proj, proj, proj, proj, log_a, g_gla, expo, masks)


def _pool_kernel(p_ref, w_ref, sc_ref, o_ref, *, seq, gw):
    row = lax.broadcasted_iota(jnp.int32, (seq, gw), 0)
    row1 = lax.broadcasted_iota(jnp.int32, (seq, 1), 0)
    for gi, w in enumerate(POOL_WINDOWS):
        cols = slice(gi * gw, (gi + 1) * gw)
        x = p_ref[:, cols].astype(F32)
        acc = x
        d = 1
        while d < w:
            acc = acc + jnp.where(row >= d, pltpu.roll(acc, d, 0), 0.0)
            d *= 2
        inv_cnt = 1.0 / jnp.minimum(row1 + 1, w).astype(F32)
        pooled = acc * inv_cnt - x
        mixed = _dot(pooled.astype(BF16), w_ref[gi]) * sc_ref[:, cols]
        o_ref[:, cols] = mixed.astype(o_ref.dtype)


def _pool(proj, w_pool, pool_scale, batch, seq):
    n = proj.shape[0]
    ng, gw, _ = w_pool.shape
    width = ng * gw
    return pl.pallas_call(
        functools.partial(_pool_kernel, seq=seq, gw=gw),
        grid=(batch,),
        in_specs=[
            pl.BlockSpec((seq, width), lambda b: (b, 0)),
            _const_spec(w_pool.shape),
            _const_spec((1, width)),
        ],
        out_specs=pl.BlockSpec((seq, width), lambda b: (b, 0)),
        out_shape=jax.ShapeDtypeStruct((n, width), BF16),
        compiler_params=_cparams("parallel"),
        name="pool",
    )(proj, w_pool, pool_scale)


def _merge_kernel(og_ref, op_ref, pg_ref, x_ref, wb_hbm, wo_hbm, gn_ref, x1_ref, h_ref,
                  wb_ref, wo_ref, stage, sem):
    @pl.when(pl.program_id(0) == 0)
    def _():
        _fetch_as_bf16(wb_hbm, wb_ref, stage, sem)
        _fetch_as_bf16(wo_hbm, wo_ref, stage, sem)

    dv = og_ref.shape[1]
    d = x_ref.shape[1]
    g0 = pg_ref.shape[1] - 2 * d
    yg = _dot(og_ref[...], wb_ref[0:dv, :])
    yp = _dot(op_ref[...], wb_ref[dv:, :])
    merged = (_sigmoid(pg_ref[:, g0:g0 + d].astype(F32)) * yg
              + _sigmoid(pg_ref[:, g0 + d:].astype(F32)) * yp)
    x1 = x_ref[...] + _dot(merged.astype(BF16), wo_ref[...])
    x1_ref[...] = x1
    h_ref[...] = _rms(x1, gn_ref[...]).astype(h_ref.dtype)


def _merge(o_gla, o_pool, proj_pg, x2d, w_branch, w_out, g_next, tm=256):
    n, d = x2d.shape
    dv = o_gla.shape[1]
    dp = o_pool.shape[1]
    wpg = proj_pg.shape[1]
    return pl.pallas_call(
        _merge_kernel,
        grid=(n // tm,),
        in_specs=[
            pl.BlockSpec((tm, dv), lambda i: (i, 0)),
            pl.BlockSpec((tm, dp), lambda i: (i, 0)),
            pl.BlockSpec((tm, wpg), lambda i: (i, 0)),
            pl.BlockSpec((tm, d), lambda i: (i, 0)),
            pl.BlockSpec(memory_space=pl.ANY),
            pl.BlockSpec(memory_space=pl.ANY),
            _const_spec((1, d)),
        ],
        out_specs=[
            pl.BlockSpec((tm, d), lambda i: (i, 0)),
            pl.BlockSpec((tm, d), lambda i: (i, 0)),
        ],
        out_shape=[
            jax.ShapeDtypeStruct((n, d), F32),
            jax.ShapeDtypeStruct((n, d), BF16),
        ],
        scratch_shapes=_resident_weight_scratch(w_branch, w_out),
        compiler_params=_cparams("arbitrary"),
        name="merge",
    )(o_gla, o_pool, proj_pg, x2d, w_branch, w_out, g_next)


def _norm_mm_kernel(x_ref, g_ref, w_ref, o_ref, wbf_ref):
    @pl.when(pl.program_id(1) == 0)
    def _():
        wbf_ref[...] = w_ref[...].astype(BF16)

    h = _rms(x_ref[...], g_ref[...]).astype(BF16)
    o_ref[...] = _dot(h, wbf_ref[...]).astype(o_ref.dtype)


def _norm_matmul(x2d, g, w, tm, tn, name):
    n, d = x2d.shape
    m = w.shape[1]
    return pl.pallas_call(
        _norm_mm_kernel,
        grid=(m // tn, n // tm),
        in_specs=[
            pl.BlockSpec((tm, d), lambda j, i: (i, 0)),
            _const_spec((1, d)),
            pl.BlockSpec((d, tn), lambda j, i: (0, j)),
        ],
        out_specs=pl.BlockSpec((tm, tn), lambda j, i: (i, j)),
        out_shape=jax.ShapeDtypeStruct((n, m), BF16),
        scratch_shapes=[pltpu.VMEM((d, tn), BF16)],
        compiler_params=_cparams("parallel", "arbitrary"),
        name=name,
    )(x2d, g, w)


def _cross_fold_kernel(kv_ref, wq_hbm, wo_hbm, wqk_ref, vw_ref, wq_ref, wo_ref, stage, sem):
    @pl.when(pl.program_id(0) == 0)
    def _():
        _fetch_as_bf16(wq_hbm, wq_ref, stage, sem)
        _fetch_as_bf16(wo_hbm, wo_ref, stage, sem)

    d = wq_ref.shape[0]
    n_mem = kv_ref.shape[0]
    hd = d // CROSS_HEADS
    for hh in range(CROSS_HEADS):
        cols = slice(hh * hd, (hh + 1) * hd)
        wqk = _dot_nt(wq_ref[:, cols], kv_ref[:, cols]) * (hd ** -0.5)
        wqk_ref[:, hh * n_mem:(hh + 1) * n_mem] = wqk.astype(wqk_ref.dtype)
        vw = _dot(kv_ref[:, d + hh * hd:d + (hh + 1) * hd], wo_ref[cols, :])
        vw_ref[hh * n_mem:(hh + 1) * n_mem, :] = vw.astype(vw_ref.dtype)


def _cross_fold(kv, w_cq, w_co, batch, n_mem):
    d = w_cq.shape[0]
    hm = CROSS_HEADS * n_mem
    return pl.pallas_call(
        _cross_fold_kernel,
        grid=(batch,),
        in_specs=[
            pl.BlockSpec((n_mem, 2 * d), lambda b: (b, 0)),
            pl.BlockSpec(memory_space=pl.ANY),
            pl.BlockSpec(memory_space=pl.ANY),
        ],
        out_specs=[
            pl.BlockSpec((d, hm), lambda b: (b, 0)),
            pl.BlockSpec((hm, d), lambda b: (b, 0)),
        ],
        out_shape=[
            jax.ShapeDtypeStruct((batch * d, hm), BF16),
            jax.ShapeDtypeStruct((batch * hm, d), BF16),
        ],
        scratch_shapes=_resident_weight_scratch(w_cq, w_co),
        compiler_params=_cparams("arbitrary"),
        name="cross_fold",
    )(kv, w_cq, w_co)


def _cross_kernel(h_ref, wqk_ref, vw_ref, x_ref, gn_ref, x2_ref, hn_ref, perm_ref):
    tm, d = h_ref.shape
    assert tm == FFN_BLOCK
    n_mem = wqk_ref.shape[1] // CROSS_HEADS
    s_all = _dot(h_ref[...], wqk_ref[...])
    probs = []
    for hh in range(CROSS_HEADS):
        s = s_all[:, hh * n_mem:(hh + 1) * n_mem]
        p = jnp.exp(s - jnp.max(s, axis=-1, keepdims=True))
        probs.append((p * (1.0 / jnp.sum(p, axis=-1, keepdims=True))).astype(BF16))
    x2 = x_ref[...] + _dot(jnp.concatenate(probs, axis=1), vw_ref[...])
    x2_ref[...] = x2
    hn = _rms(x2, gn_ref[...])
    for k in range(FFN_GROUPS):
        t0 = k * SUBLANES
        first = SUBLANES * (t0 % FFN_GROUPS) + t0 // FFN_GROUPS
        for c in range(perm_ref.shape[0]):
            perm_ref[c, pl.ds(first, SUBLANES, stride=SUBLANES), :] = (
                hn[t0:t0 + SUBLANES, c * LANES:(c + 1) * LANES])
    hn_ref[...] = jnp.concatenate([perm_ref[c] for c in range(perm_ref.shape[0])],
                                  axis=1).astype(hn_ref.dtype)


def _cross(h, wqk, vw, x2d, g_next, batch, seq, tm=FFN_BLOCK):
    n, d = x2d.shape
    hm = wqk.shape[1]
    spb = seq // tm
    return pl.pallas_call(
        _cross_kernel,
        grid=(batch, spb),
        in_specs=[
            pl.BlockSpec((tm, d), lambda b, i: (b * spb + i, 0)),
            pl.BlockSpec((d, hm), lambda b, i: (b, 0)),
            pl.BlockSpec((hm, d), lambda b, i: (b, 0)),
            pl.BlockSpec((tm, d), lambda b, i: (b * spb + i, 0)),
            _const_spec((1, d)),
        ],
        out_specs=[
            pl.BlockSpec((tm, d), lambda b, i: (b * spb + i, 0)),
            pl.BlockSpec((tm, d), lambda b, i: (b * spb + i, 0)),
        ],
        out_shape=[
            jax.ShapeDtypeStruct((n, d), F32),
            jax.ShapeDtypeStruct((n, d), BF16),
        ],
        scratch_shapes=[pltpu.VMEM((d // LANES, tm, LANES), F32)],
        compiler_params=_cparams("parallel", "parallel"),
        name="cross",
    )(h, wqk, vw, x2d, g_next)


def _ffn_up_kernel(h_ref, wg32_ref, wv32_ref, cwg_ref, cwv_ref, cbg_ref, cbv_ref, act_ref,
                   wg_ref, wv_ref, *, rb):
    seq, tn = act_ref.shape
    row = lax.broadcasted_iota(jnp.int32, (SUBLANES, tn), 0)

    @pl.when(pl.program_id(1) == 0)
    def _():
        wg_ref[...] = wg32_ref[...].astype(BF16)
        wv_ref[...] = wv32_ref[...].astype(BF16)

    def wrap(group, prev_group):
        return jnp.where(row == 0, prev_group[SUBLANES - 1:SUBLANES], pltpu.roll(group, 1, 0))

    def conv(u, prev, cw_ref, cb_ref):
        w30 = wrap(u[rb - 2 * SUBLANES:rb - SUBLANES], prev[:SUBLANES])
        w31 = wrap(u[rb - SUBLANES:], prev[SUBLANES:])
        u1 = jnp.concatenate([w31, u[:rb - SUBLANES]], axis=0)
        u2 = jnp.concatenate([w30, w31, u[:rb - 2 * SUBLANES]], axis=0)
        return cw_ref[2:3] * u + cw_ref[1:2] * u1 + cw_ref[0:1] * u2 + cb_ref[...]

    prev_g = jnp.zeros((2 * SUBLANES, tn), F32)
    prev_v = jnp.zeros((2 * SUBLANES, tn), F32)
    for r0 in range(0, seq, rb):
        hb = h_ref[r0:r0 + rb, :]
        ug = _dot(hb, wg_ref[...])
        uv = _dot(hb, wv_ref[...])
        gate = conv(ug, prev_g, cwg_ref, cbg_ref)
        val = conv(uv, prev_v, cwv_ref, cbv_ref)
        prev_g = ug[rb - 2 * SUBLANES:]
        prev_v = uv[rb - 2 * SUBLANES:]
        act_ref[r0:r0 + rb, :] = (gate * _sigmoid(gate) * val).astype(act_ref.dtype)


def _ffn_up(h, w_up, conv_w, conv_b, batch, seq, tn=512, rb=FFN_BLOCK):
    n, d = h.shape
    dff = w_up.shape[1] // 2
    nj = dff // tn
    return pl.pallas_call(
        functools.partial(_ffn_up_kernel, rb=rb),
        grid=(nj, batch),
        in_specs=[
            pl.BlockSpec((seq, d), lambda j, b: (b, 0)),
            pl.BlockSpec((d, tn), lambda j, b: (0, j)),
            pl.BlockSpec((d, tn), lambda j, b: (0, nj + j)),
            pl.BlockSpec((CONV_W, tn), lambda j, b: (0, j)),
            pl.BlockSpec((CONV_W, tn), lambda j, b: (0, nj + j)),
            pl.BlockSpec((1, tn), lambda j, b: (0, j)),
            pl.BlockSpec((1, tn), lambda j, b: (0, nj + j)),
        ],
        out_specs=pl.BlockSpec((seq, tn), lambda j, b: (b, j)),
        out_shape=jax.ShapeDtypeStruct((n, dff), BF16),
        scratch_shapes=[pltpu.VMEM((d, tn), BF16), pltpu.VMEM((d, tn), BF16)],
        compiler_params=_cparams("parallel", "arbitrary"),
        name="ffn_up",
    )(h, w_up, w_up, conv_w, conv_w, conv_b, conv_b)


def _ffn_down_kernel(a_ref, w_hbm, x_ref, g_ref, o_ref, y_ref, w_ref, stage, sem):
    assert a_ref.shape[0] == FFN_BLOCK

    @pl.when(pl.program_id(0) == 0)
    def _():
        _fetch_as_bf16(w_hbm, w_ref, stage, sem)

    _store_lane_chunks(y_ref, _dot(a_ref[...], w_ref[...]))
    for k in range(FFN_GROUPS):
        t0 = k * SUBLANES
        first = SUBLANES * (t0 % FFN_GROUPS) + t0 // FFN_GROUPS
        y = _load_strided_rows(y_ref, first, SUBLANES)
        rows = slice(t0, t0 + SUBLANES)
        o_ref[rows, :] = _rms(x_ref[rows, :] + y, g_ref[...])


def _ffn_down(act, w_down, x2d, g, tm=FFN_BLOCK):
    n, d = x2d.shape
    dff = act.shape[1]
    return pl.pallas_call(
        _ffn_down_kernel,
        grid=(n // tm,),
        in_specs=[
            pl.BlockSpec((tm, dff), lambda i: (i, 0)),
            pl.BlockSpec(memory_space=pl.ANY),
            pl.BlockSpec((tm, d), lambda i: (i, 0)),
            _const_spec((1, d)),
        ],
        out_specs=pl.BlockSpec((tm, d), lambda i: (i, 0)),
        out_shape=jax.ShapeDtypeStruct((n, d), F32),
        scratch_shapes=([pltpu.VMEM((d // LANES, tm, LANES), F32)]
                        + _resident_weight_scratch(w_down)),
        compiler_params=_cparams("arbitrary"),
        name="ffn_down",
    )(act, w_down, x2d, g)


def kernel(x, mem, g_mix, w_in, w_a2, b_a, g_gla, w_pool, pool_scale, w_branch, w_out,
           g_cross, g_mem, w_cq, w_ckv, w_co, g_ffn, w_up, conv_w, conv_b, w_down, g_final):
    batch, seq, d = x.shape
    n_mem = mem.shape[1]
    depth = w_in.shape[0]
    assert depth == 1
    dk = w_a2.shape[2]
    dv = g_gla.shape[1]
    off_a = 2 * dk + 2 * dv
    off_p = off_a + GLA_GATE_RANK
    row2 = lambda v: v.reshape(1, -1)

    xs = x.reshape(batch * seq, d)
    mem2d = mem.reshape(batch * n_mem, d)

    w_in_t = jnp.swapaxes(w_in[0], 0, 1)
    d_in = w_in_t.shape[0]
    wa2 = jnp.pad(w_a2[0], ((0, LANES - GLA_GATE_RANK), (0, 0))).astype(BF16)

    h, log_a = _norm_gate(xs, row2(g_mix[0]), w_in_t, off_a, wa2, row2(b_a[0]))
    proj_qkvr = _matmul_nt(h, w_in_t, 0, off_a, 1024, 1024, BF16, "in_proj_qkvr")
    proj_pg = _matmul_nt(h, w_in_t, off_p, d_in - off_p, 1024, 1024, BF16, "in_proj_pg")
    o_gla = _gla(proj_qkvr, log_a, row2(g_gla[0]), batch, seq, dk, dv)
    o_pool = _pool(proj_pg, w_pool[0].astype(BF16), row2(pool_scale[0]), batch, seq)
    x1, h2 = _merge(o_gla, o_pool, proj_pg, xs, w_branch[0], w_out[0],
                    row2(g_cross[0]))
    kv = _norm_matmul(mem2d, row2(g_mem[0]), w_ckv[0], 1024, 1024, "mem_kv")
    wqk, vw = _cross_fold(kv, w_cq[0], w_co[0], batch, n_mem)
    x2, h3 = _cross(h2, wqk, vw, x1, row2(g_ffn[0]), batch, seq)
    act = _ffn_up(h3, w_up[0], conv_w[0], row2(conv_b[0]), batch, seq)
    out = _ffn_down(act, w_down[0], x2, row2(g_final))
    return out.reshape(batch, seq, d)
```

```python
import functools

import numpy as np
import jax
import jax.numpy as jnp
from jax import lax
from jax.experimental import pallas as pl
from jax.experimental.pallas import tpu as pltpu

F32 = jnp.float32
BF16 = jnp.bfloat16

EPS = 1e-6
LOG2_E = 1.4426950408889634
GLA_HEADS = 4
GLA_GATE_RANK = 16
GLA_GATE_NORM = 16.0
GLA_CHUNK = 64
POOL_WINDOWS = (2, 4, 8, 16)
CROSS_HEADS = 4
CONV_W = 3

LANES = 128
SUBLANES = 8
VMEM_LIMIT = 52 * 1024 * 1024

FFN_BLOCK = 256
FFN_GROUPS = FFN_BLOCK // SUBLANES

GLA_LEVELS = (32, 16, 8, 4, 2, 1)
GLA_SPLIT = 3


def _cparams(*sem):
    return pltpu.CompilerParams(dimension_semantics=sem, vmem_limit_bytes=VMEM_LIMIT)


def _const_spec(shape):
    nd = len(shape)
    return pl.BlockSpec(shape, lambda *_: (0,) * nd, pipeline_mode=pl.Buffered(1))


def _rms(xf, g):
    return xf * lax.rsqrt(jnp.mean(xf * xf, axis=-1, keepdims=True) + EPS) * g


def _sigmoid(x):
    return 1.0 / (1.0 + jnp.exp(-x))


def _store_lane_chunks(ref, value):
    for c in range(ref.shape[0]):
        ref[c] = value[:, c * LANES:(c + 1) * LANES]


def _load_strided_rows(ref, first, stride):
    return jnp.concatenate([ref[c, pl.ds(first, SUBLANES, stride=stride), :]
                            for c in range(ref.shape[0])], axis=1)


STAGE_ROWS = 256


def _fetch_as_bf16(w_hbm, w_bf, stage, sem):
    rows = stage.shape[1]
    n_chunks = w_hbm.shape[0] // rows

    def copy(k):
        return pltpu.make_async_copy(w_hbm.at[pl.ds(k * rows, rows), :], stage.at[k % 2],
                                     sem.at[k % 2])

    copy(0).start()
    for k in range(n_chunks):
        if k + 1 < n_chunks:
            copy(k + 1).start()
        copy(k).wait()
        w_bf[k * rows:(k + 1) * rows, :] = stage[k % 2].astype(BF16)


def _resident_weight_scratch(*weights):
    cols = weights[0].shape[1]
    assert all(w.shape[1] == cols and w.shape[0] % STAGE_ROWS == 0 for w in weights)
    return ([pltpu.VMEM(w.shape, BF16) for w in weights]
            + [pltpu.VMEM((2, STAGE_ROWS, cols), F32), pltpu.SemaphoreType.DMA((2,))])


def _dot(a, b):
    return jnp.dot(a, b, preferred_element_type=F32)


def _dot_nt(a, b):
    return lax.dot_general(a, b, (((1,), (1,)), ((), ())), preferred_element_type=F32)


def _dot_tn(a, b):
    return lax.dot_general(a, b, (((0,), (0,)), ((), ())), preferred_element_type=F32)


def _norm_gate_kernel(x_ref, g_ref, wa1_ref, wa2_ref, ba_ref, h_ref, la_ref):
    h = _rms(x_ref[...], g_ref[...]).astype(BF16)
    h_ref[...] = h
    a = _dot_nt(h, wa1_ref[...].astype(BF16))
    gp = _dot(a.astype(BF16), wa2_ref[...]) + ba_ref[...]
    ls = jnp.minimum(gp, 0.0) - jnp.log1p(jnp.exp(-jnp.abs(gp)))
    la_ref[...] = ls * (1.0 / GLA_GATE_NORM)


def _norm_gate(x2d, g, w_in_t, gate_row0, wa2, ba, tm=1024):
    n, d = x2d.shape
    dk = wa2.shape[1]
    return pl.pallas_call(
        _norm_gate_kernel,
        grid=(n // tm,),
        in_specs=[
            pl.BlockSpec((tm, d), lambda i: (i, 0)),
            _const_spec((1, d)),
            pl.BlockSpec((LANES, d), lambda i: (gate_row0 // LANES, 0), pipeline_mode=pl.Buffered(1)),
            _const_spec(wa2.shape),
            _const_spec((1, dk)),
        ],
        out_specs=[
            pl.BlockSpec((tm, d), lambda i: (i, 0)),
            pl.BlockSpec((tm, dk), lambda i: (i, 0)),
        ],
        out_shape=[
            jax.ShapeDtypeStruct((n, d), BF16),
            jax.ShapeDtypeStruct((n, dk), F32),
        ],
        compiler_params=_cparams("parallel"),
        name="norm_gate",
    )(x2d, g, w_in_t, wa2, ba)


def _mm_nt_kernel(a_ref, wt_ref, o_ref, wbf_ref):
    @pl.when(pl.program_id(1) == 0)
    def _():
        wbf_ref[...] = wt_ref[...].astype(BF16)

    o_ref[...] = _dot_nt(a_ref[...], wbf_ref[...]).astype(o_ref.dtype)


def _matmul_nt(a, wt, row0, m, tm, tn, out_dtype, name):
    n, k = a.shape
    return pl.pallas_call(
        _mm_nt_kernel,
        grid=(m // tn, n // tm),
        in_specs=[
            pl.BlockSpec((tm, k), lambda j, i: (i, 0)),
            pl.BlockSpec((pl.Element(tn), pl.Element(k)),
                         lambda j, i: (pl.multiple_of(row0 + j * tn, SUBLANES), 0)),
        ],
        out_specs=pl.BlockSpec((tm, tn), lambda j, i: (i, j)),
        out_shape=jax.ShapeDtypeStruct((n, m), out_dtype),
        scratch_shapes=[pltpu.VMEM((tn, k), BF16)],
        compiler_params=_cparams("parallel", "arbitrary"),
        name=name,
    )(a, wt)


def _gla_consts():
    c = GLA_CHUNK
    i = np.arange(c)[:, None]
    t = np.arange(c)[None, :]
    blocks = [t <= i, t > i]
    masks = []
    for s in GLA_LEVELS:
        upper = (i % (2 * s)) >= s
        blk_start = i - (i % s)
        blk_end = blk_start + s - 1
        blocks.append((upper & (t >= blk_start) & (t <= i)) | (~upper & (t > i) & (t <= blk_end)))
        same_group = (i // (2 * s)) == (t // (2 * s))
        masks.append(same_group & upper & ((t % (2 * s)) < s))
    masks.append(i == t)
    assert (np.sum(masks, axis=0) == (t <= i)).all()
    expo = np.concatenate(blocks, axis=0).astype(np.float32)
    expo = np.concatenate([expo] * GLA_SPLIT, axis=1)
    return jnp.asarray(expo, BF16), jnp.asarray(np.stack(masks).astype(np.float32))


def _gla_kernel(q_ref, k_ref, v_ref, r_ref, la_ref, gg_ref, expo_ref, mask_ref, o_ref, st_ref,
                att_ref, qin_ref, kout_ref, elast_ref, *, q_scale):
    c = GLA_CHUNK
    n_levels = len(GLA_LEVELS)
    hk = q_ref.shape[1] // GLA_HEADS
    hv = v_ref.shape[1] // GLA_HEADS

    @pl.when(pl.program_id(1) == 0)
    def _():
        st_ref[...] = jnp.zeros_like(st_ref)

    def chunk(ci, carry):
        r0 = pl.multiple_of(ci * c, c)
        rows = pl.ds(r0, c)
        g = la_ref[rows, :] * LOG2_E
        g1 = g.astype(BF16)
        rem = g - g1.astype(F32)
        g2 = rem.astype(BF16)
        g3 = (rem - g2.astype(F32)).astype(BF16)
        dec_all = jnp.exp2(_dot(expo_ref[...], jnp.concatenate([g1, g2, g3], axis=0)))
        for hh in range(GLA_HEADS):
            kc = slice(hh * hk, (hh + 1) * hk)
            dec = dec_all[:, kc]

            q = q_ref[rows, kc].astype(F32) * q_scale
            k = k_ref[rows, kc].astype(F32)

            att = mask_ref[n_levels] * jnp.sum(q * k, axis=-1, keepdims=True)
            for l in range(n_levels):
                f = dec[(2 + l) * c:(3 + l) * c]
                att = att + mask_ref[l] * _dot_nt((q * f).astype(BF16), (k * f).astype(BF16))

            att_ref[ci, hh] = att.astype(BF16)
            qin_ref[rows, kc] = (q * dec[0:c]).astype(BF16)
            kout_ref[rows, kc] = (k * dec[c:2 * c]).astype(BF16)
            elast_ref[ci, hh] = dec[c - SUBLANES:c]
        return carry

    def scan(ci, carry):
        r0 = pl.multiple_of(ci * c, c)
        rows = pl.ds(r0, c)
        for hh in range(GLA_HEADS):
            kc = slice(hh * hk, (hh + 1) * hk)
            vc = slice(hh * hv, (hh + 1) * hv)
            v = v_ref[rows, vc]
            st = st_ref[hh]
            o = _dot(att_ref[ci, hh], v) + _dot_nt(qin_ref[rows, kc], st.astype(BF16))
            st_ref[hh] = (st * elast_ref[ci, hh][SUBLANES - 1:SUBLANES]
                          + _dot_tn(v, kout_ref[rows, kc]))

            on = o * lax.rsqrt(jnp.mean(o * o, axis=-1, keepdims=True) + EPS) * gg_ref[:, vc]
            r = r_ref[rows, vc].astype(F32)
            o_ref[rows, vc] = (on * (r * _sigmoid(r))).astype(o_ref.dtype)
        return carry

    n_chunks = q_ref.shape[0] // c
    lax.fori_loop(0, n_chunks, chunk, 0, unroll=8)
    lax.fori_loop(0, n_chunks, scan, 0, unroll=8)


def _gla(proj, log_a, g_gla, batch, seq, dk, dv, tt=1024):
    n = proj.shape[0]
    hk = dk // GLA_HEADS
    hv = dv // GLA_HEADS
    expo, masks = _gla_consts()
    spb = seq // tt
    v_blk = 2 * dk // dv
    return pl.pallas_call(
        functools.partial(_gla_kernel, q_scale=hk ** -0.5),
        grid=(batch, spb),
        in_specs=[
            pl.BlockSpec((tt, dk), lambda b, s: (b * spb + s, 0)),
            pl.BlockSpec((tt, dk), lambda b, s: (b * spb + s, 1)),
            pl.BlockSpec((tt, dv), lambda b, s: (b * spb + s, v_blk)),
            pl.BlockSpec((tt, dv), lambda b, s: (b * spb + s, v_blk + 1)),
            pl.BlockSpec((tt, dk), lambda b, s: (b * spb + s, 0)),
            _const_spec((1, dv)),
            _const_spec(expo.shape),
            _const_spec(masks.shape),
        ],
        out_specs=pl.BlockSpec((tt, dv), lambda b, s: (b * spb + s, 0)),
        out_shape=jax.ShapeDtypeStruct((n, dv), BF16),
        scratch_shapes=[
            pltpu.VMEM((GLA_HEADS, hv, hk), F32),
            pltpu.VMEM((tt // GLA_CHUNK, GLA_HEADS, GLA_CHUNK, GLA_CHUNK), BF16),
            pltpu.VMEM((tt, dk), BF16),
            pltpu.VMEM((tt, dk), BF16),
            pltpu.VMEM((tt // GLA_CHUNK, GLA_HEADS, SUBLANES, hk), F32),
        ],
        compiler_params=_cparams("parallel", "arbitrary"),
        name="gla",
    )(proj, proj, proj, proj, log_a, g_gla, expo, masks)


def _pool_kernel(p_ref, w_ref, sc_ref, o_ref, *, seq, gw):
    row = lax.broadcasted_iota(jnp.int32, (seq, gw), 0)
    row1 = lax.broadcasted_iota(jnp.int32, (seq, 1), 0)
    for gi, w in enumerate(POOL_WINDOWS):
        cols = slice(gi * gw, (gi + 1) * gw)
        x = p_ref[:, cols].astype(F32)
        acc = x
        d = 1
        while d < w:
            acc = acc + jnp.where(row >= d, pltpu.roll(acc, d, 0), 0.0)
            d *= 2
        inv_cnt = 1.0 / jnp.minimum(row1 + 1, w).astype(F32)
        pooled = acc * inv_cnt - x
        mixed = _dot(pooled.astype(BF16), w_ref[gi]) * sc_ref[:, cols]
        o_ref[:, cols] = mixed.astype(o_ref.dtype)


def _pool(proj, w_pool, pool_scale, batch, seq):
    n = proj.shape[0]
    ng, gw, _ = w_pool.shape
    width = ng * gw
    return pl.pallas_call(
        functools.partial(_pool_kernel, seq=seq, gw=gw),
        grid=(batch,),
        in_specs=[
            pl.BlockSpec((seq, width), lambda b: (b, 0)),
            _const_spec(w_pool.shape),
            _const_spec((1, width)),
        ],
        out_specs=pl.BlockSpec((seq, width), lambda b: (b, 0)),
        out_shape=jax.ShapeDtypeStruct((n, width), BF16),
        compiler_params=_cparams("parallel"),
        name="pool",
    )(proj, w_pool, pool_scale)


def _merge_kernel(og_ref, op_ref, pg_ref, x_ref, wb_hbm, wo_hbm, gn_ref, x1_ref, h_ref,
                  wb_ref, wo_ref, stage, sem):
    @pl.when(pl.program_id(0) == 0)
    def _():
        _fetch_as_bf16(wb_hbm, wb_ref, stage, sem)
        _fetch_as_bf16(wo_hbm, wo_ref, stage, sem)

    dv = og_ref.shape[1]
    d = x_ref.shape[1]
    g0 = pg_ref.shape[1] - 2 * d
    yg = _dot(og_ref[...], wb_ref[0:dv, :])
    yp = _dot(op_ref[...], wb_ref[dv:, :])
    merged = (_sigmoid(pg_ref[:, g0:g0 + d].astype(F32)) * yg
              + _sigmoid(pg_ref[:, g0 + d:].astype(F32)) * yp)
    x1 = x_ref[...] + _dot(merged.astype(BF16), wo_ref[...])
    x1_ref[...] = x1
    h_ref[...] = _rms(x1, gn_ref[...]).astype(h_ref.dtype)


def _merge(o_gla, o_pool, proj_pg, x2d, w_branch, w_out, g_next, tm=256):
    n, d = x2d.shape
    dv = o_gla.shape[1]
    dp = o_pool.shape[1]
    wpg = proj_pg.shape[1]
    return pl.pallas_call(
        _merge_kernel,
        grid=(n // tm,),
        in_specs=[
            pl.BlockSpec((tm, dv), lambda i: (i, 0)),
            pl.BlockSpec((tm, dp), lambda i: (i, 0)),
            pl.BlockSpec((tm, wpg), lambda i: (i, 0)),
            pl.BlockSpec((tm, d), lambda i: (i, 0)),
            pl.BlockSpec(memory_space=pl.ANY),
            pl.BlockSpec(memory_space=pl.ANY),
            _const_spec((1, d)),
        ],
        out_specs=[
            pl.BlockSpec((tm, d), lambda i: (i, 0)),
            pl.BlockSpec((tm, d), lambda i: (i, 0)),
        ],
        out_shape=[
            jax.ShapeDtypeStruct((n, d), F32),
            jax.ShapeDtypeStruct((n, d), BF16),
        ],
        scratch_shapes=_resident_weight_scratch(w_branch, w_out),
        compiler_params=_cparams("arbitrary"),
        name="merge",
    )(o_gla, o_pool, proj_pg, x2d, w_branch, w_out, g_next)


def _norm_mm_kernel(x_ref, g_ref, w_ref, o_ref, wbf_ref):
    @pl.when(pl.program_id(1) == 0)
    def _():
        wbf_ref[...] = w_ref[...].astype(BF16)

    h = _rms(x_ref[...], g_ref[...]).astype(BF16)
    o_ref[...] = _dot(h, wbf_ref[...]).astype(o_ref.dtype)


def _norm_matmul(x2d, g, w, tm, tn, name):
    n, d = x2d.shape
    m = w.shape[1]
    return pl.pallas_call(
        _norm_mm_kernel,
        grid=(m // tn, n // tm),
        in_specs=[
            pl.BlockSpec((tm, d), lambda j, i: (i, 0)),
            _const_spec((1, d)),
            pl.BlockSpec((d, tn), lambda j, i: (0, j)),
        ],
        out_specs=pl.BlockSpec((tm, tn), lambda j, i: (i, j)),
        out_shape=jax.ShapeDtypeStruct((n, m), BF16),
        scratch_shapes=[pltpu.VMEM((d, tn), BF16)],
        compiler_params=_cparams("parallel", "arbitrary"),
        name=name,
    )(x2d, g, w)


def _cross_fold_kernel(kv_ref, wq_hbm, wo_hbm, wqk_ref, vw_ref, wq_ref, wo_ref, stage, sem):
    @pl.when(pl.program_id(0) == 0)
    def _():
        _fetch_as_bf16(wq_hbm, wq_ref, stage, sem)
        _fetch_as_bf16(wo_hbm, wo_ref, stage, sem)

    d = wq_ref.shape[0]
    n_mem = kv_ref.shape[0]
    hd = d // CROSS_HEADS
    for hh in range(CROSS_HEADS):
        cols = slice(hh * hd, (hh + 1) * hd)
        wqk = _dot_nt(wq_ref[:, cols], kv_ref[:, cols]) * (hd ** -0.5)
        wqk_ref[:, hh * n_mem:(hh + 1) * n_mem] = wqk.astype(wqk_ref.dtype)
        vw = _dot(kv_ref[:, d + hh * hd:d + (hh + 1) * hd], wo_ref[cols, :])
        vw_ref[hh * n_mem:(hh + 1) * n_mem, :] = vw.astype(vw_ref.dtype)


def _cross_fold(kv, w_cq, w_co, batch, n_mem):
    d = w_cq.shape[0]
    hm = CROSS_HEADS * n_mem
    return pl.pallas_call(
        _cross_fold_kernel,
        grid=(batch,),
        in_specs=[
            pl.BlockSpec((n_mem, 2 * d), lambda b: (b, 0)),
            pl.BlockSpec(memory_space=pl.ANY),
            pl.BlockSpec(memory_space=pl.ANY),
        ],
        out_specs=[
            pl.BlockSpec((d, hm), lambda b: (b, 0)),
            pl.BlockSpec((hm, d), lambda b: (b, 0)),
        ],
        out_shape=[
            jax.ShapeDtypeStruct((batch * d, hm), BF16),
            jax.ShapeDtypeStruct((batch * hm, d), BF16),
        ],
        scratch_shapes=_resident_weight_scratch(w_cq, w_co),
        compiler_params=_cparams("arbitrary"),
        name="cross_fold",
    )(kv, w_cq, w_co)


def _cross_kernel(h_ref, wqk_ref, vw_ref, x_ref, gn_ref, x2_ref, hn_ref, perm_ref):
    tm, d = h_ref.shape
    assert tm == FFN_BLOCK
    n_mem = wqk_ref.shape[1] // CROSS_HEADS
    s_all = _dot(h_ref[...], wqk_ref[...])
    probs = []
    for hh in range(CROSS_HEADS):
        s = s_all[:, hh * n_mem:(hh + 1) * n_mem]
        p = jnp.exp(s - jnp.max(s, axis=-1, keepdims=True))
        probs.append((p * (1.0 / jnp.sum(p, axis=-1, keepdims=True))).astype(BF16))
    x2 = x_ref[...] + _dot(jnp.concatenate(probs, axis=1), vw_ref[...])
    x2_ref[...] = x2
    hn = _rms(x2, gn_ref[...])
    for k in range(FFN_GROUPS):
        t0 = k * SUBLANES
        first = SUBLANES * (t0 % FFN_GROUPS) + t0 // FFN_GROUPS
        for c in range(perm_ref.shape[0]):
            perm_ref[c, pl.ds(first, SUBLANES, stride=SUBLANES), :] = (
                hn[t0:t0 + SUBLANES, c * LANES:(c + 1) * LANES])
    hn_ref[...] = jnp.concatenate([perm_ref[c] for c in range(perm_ref.shape[0])],
                                  axis=1).astype(hn_ref.dtype)


def _cross(h, wqk, vw, x2d, g_next, batch, seq, tm=FFN_BLOCK):
    n, d = x2d.shape
    hm = wqk.shape[1]
    spb = seq // tm
    return pl.pallas_call(
        _cross_kernel,
        grid=(batch, spb),
        in_specs=[
            pl.BlockSpec((tm, d), lambda b, i: (b * spb + i, 0)),
            pl.BlockSpec((d, hm), lambda b, i: (b, 0)),
            pl.BlockSpec((hm, d), lambda b, i: (b, 0)),
            pl.BlockSpec((tm, d), lambda b, i: (b * spb + i, 0)),
            _const_spec((1, d)),
        ],
        out_specs=[
            pl.BlockSpec((tm, d), lambda b, i: (b * spb + i, 0)),
            pl.BlockSpec((tm, d), lambda b, i: (b * spb + i, 0)),
        ],
        out_shape=[
            jax.ShapeDtypeStruct((n, d), F32),
            jax.ShapeDtypeStruct((n, d), BF16),
        ],
        scratch_shapes=[pltpu.VMEM((d // LANES, tm, LANES), F32)],
        compiler_params=_cparams("parallel", "parallel"),
        name="cross",
    )(h, wqk, vw, x2d, g_next)


def _ffn_up_kernel(h_ref, wg32_ref, wv32_ref, cwg_ref, cwv_ref, cbg_ref, cbv_ref, act_ref,
                   wg_ref, wv_ref, *, rb):
    seq, tn = act_ref.shape
    row = lax.broadcasted_iota(jnp.int32, (SUBLANES, tn), 0)

    @pl.when(pl.program_id(1) == 0)
    def _():
        wg_ref[...] = wg32_ref[...].astype(BF16)
        wv_ref[...] = wv32_ref[...].astype(BF16)

    def wrap(group, prev_group):
        return jnp.where(row == 0, prev_group[SUBLANES - 1:SUBLANES], pltpu.roll(group, 1, 0))

    def conv(u, prev, cw_ref, cb_ref):
        w30 = wrap(u[rb - 2 * SUBLANES:rb - SUBLANES], prev[:SUBLANES])
        w31 = wrap(u[rb - SUBLANES:], prev[SUBLANES:])
        u1 = jnp.concatenate([w31, u[:rb - SUBLANES]], axis=0)
        u2 = jnp.concatenate([w30, w31, u[:rb - 2 * SUBLANES]], axis=0)
        return cw_ref[2:3] * u + cw_ref[1:2] * u1 + cw_ref[0:1] * u2 + cb_ref[...]

    prev_g = jnp.zeros((2 * SUBLANES, tn), F32)
    prev_v = jnp.zeros((2 * SUBLANES, tn), F32)
    for r0 in range(0, seq, rb):
        hb = h_ref[r0:r0 + rb, :]
        ug = _dot(hb, wg_ref[...])
        uv = _dot(hb, wv_ref[...])
        gate = conv(ug, prev_g, cwg_ref, cbg_ref)
        val = conv(uv, prev_v, cwv_ref, cbv_ref)
        prev_g = ug[rb - 2 * SUBLANES:]
        prev_v = uv[rb - 2 * SUBLANES:]
        act_ref[r0:r0 + rb, :] = (gate * _sigmoid(gate) * val).astype(act_ref.dtype)


def _ffn_up(h, w_up, conv_w, conv_b, batch, seq, tn=512, rb=FFN_BLOCK):
    n, d = h.shape
    dff = w_up.shape[1] // 2
    nj = dff // tn
    return pl.pallas_call(
        functools.partial(_ffn_up_kernel, rb=rb),
        grid=(nj, batch),
        in_specs=[
            pl.BlockSpec((seq, d), lambda j, b: (b, 0)),
            pl.BlockSpec((d, tn), lambda j, b: (0, j)),
            pl.BlockSpec((d, tn), lambda j, b: (0, nj + j)),
            pl.BlockSpec((CONV_W, tn), lambda j, b: (0, j)),
            pl.BlockSpec((CONV_W, tn), lambda j, b: (0, nj + j)),
            pl.BlockSpec((1, tn), lambda j, b: (0, j)),
            pl.BlockSpec((1, tn), lambda j, b: (0, nj + j)),
        ],
        out_specs=pl.BlockSpec((seq, tn), lambda j, b: (b, j)),
        out_shape=jax.ShapeDtypeStruct((n, dff), BF16),
        scratch_shapes=[pltpu.VMEM((d, tn), BF16), pltpu.VMEM((d, tn), BF16)],
        compiler_params=_cparams("parallel", "arbitrary"),
        name="ffn_up",
    )(h, w_up, w_up, conv_w, conv_w, conv_b, conv_b)


def _ffn_down_kernel(a_ref, w_hbm, x_ref, g_ref, o_ref, y_ref, w_ref, stage, sem):
    assert a_ref.shape[0] == FFN_BLOCK

    @pl.when(pl.program_id(0) == 0)
    def _():
        _fetch_as_bf16(w_hbm, w_ref, stage, sem)

    _store_lane_chunks(y_ref, _dot(a_ref[...], w_ref[...]))
    for k in range(FFN_GROUPS):
        t0 = k * SUBLANES
        first = SUBLANES * (t0 % FFN_GROUPS) + t0 // FFN_GROUPS
        y = _load_strided_rows(y_ref, first, SUBLANES)
        rows = slice(t0, t0 + SUBLANES)
        o_ref[rows, :] = _rms(x_ref[rows, :] + y, g_ref[...])


def _ffn_down(act, w_down, x2d, g, tm=FFN_BLOCK):
    n, d = x2d.shape
    dff = act.shape[1]
    return pl.pallas_call(
        _ffn_down_kernel,
        grid=(n // tm,),
        in_specs=[
            pl.BlockSpec((tm, dff), lambda i: (i, 0)),
            pl.BlockSpec(memory_space=pl.ANY),
            pl.BlockSpec((tm, d), lambda i: (i, 0)),
            _const_spec((1, d)),
        ],
        out_specs=pl.BlockSpec((tm, d), lambda i: (i, 0)),
        out_shape=jax.ShapeDtypeStruct((n, d), F32),
        scratch_shapes=([pltpu.VMEM((d // LANES, tm, LANES), F32)]
                        + _resident_weight_scratch(w_down)),
        compiler_params=_cparams("arbitrary"),
        name="ffn_down",
    )(act, w_down, x2d, g)


def kernel(x, mem, g_mix, w_in, w_a2, b_a, g_gla, w_pool, pool_scale, w_branch, w_out,
           g_cross, g_mem, w_cq, w_ckv, w_co, g_ffn, w_up, conv_w, conv_b, w_down, g_final):
    batch, seq, d = x.shape
    n_mem = mem.shape[1]
    depth = w_in.shape[0]
    assert depth == 1
    dk = w_a2.shape[2]
    dv = g_gla.shape[1]
    off_a = 2 * dk + 2 * dv
    off_p = off_a + GLA_GATE_RANK
    row2 = lambda v: v.reshape(1, -1)

    xs = x.reshape(batch * seq, d)
    mem2d = mem.reshape(batch * n_mem, d)

    w_in_t = jnp.swapaxes(w_in[0], 0, 1)
    d_in = w_in_t.shape[0]
    wa2 = jnp.pad(w_a2[0], ((0, LANES - GLA_GATE_RANK), (0, 0))).astype(BF16)

    h, log_a = _norm_gate(xs, row2(g_mix[0]), w_in_t, off_a, wa2, row2(b_a[0]))
    proj_qkvr = _matmul_nt(h, w_in_t, 0, off_a, 1024, 1024, BF16, "in_proj_qkvr")
    proj_pg = _matmul_nt(h, w_in_t, off_p, d_in - off_p, 1024, 1024, BF16, "in_proj_pg")
    o_gla = _gla(proj_qkvr, log_a, row2(g_gla[0]), batch, seq, dk, dv)
    o_pool = _pool(proj_pg, w_pool[0].astype(BF16), row2(pool_scale[0]), batch, seq)
    x1, h2 = _merge(o_gla, o_pool, proj_pg, xs, w_branch[0], w_out[0],
                    row2(g_cross[0]))
    kv = _norm_matmul(mem2d, row2(g_mem[0]), w_ckv[0], 1024, 1024, "mem_kv")
    wqk, vw = _cross_fold(kv, w_cq[0], w_co[0], batch, n_mem)
    x2, h3 = _cross(h2, wqk, vw, x1, row2(g_ffn[0]), batch, seq)
    act = _ffn_up(h3, w_up[0], conv_w[0], row2(conv_b[0]), batch, seq)
    out = _ffn_down(act, w_down[0], x2, row2(g_final))
    return out.reshape(batch, seq, d)
```

```python
import functools

import numpy as np
import jax
import jax.numpy as jnp
from jax import lax
from jax.experimental import pallas as pl
from jax.experimental.pallas import tpu as pltpu

F32 = jnp.float32
BF16 = jnp.bfloat16

EPS = 1e-6
LOG2_E = 1.4426950408889634
GLA_HEADS = 4
GLA_GATE_RANK = 16
GLA_GATE_NORM = 16.0
GLA_CHUNK = 64
POOL_WINDOWS = (2, 4, 8, 16)
CROSS_HEADS = 4
CONV_W = 3

LANES = 128
SUBLANES = 8
VMEM_LIMIT = 52 * 1024 * 1024

FFN_BLOCK = 256
FFN_GROUPS = FFN_BLOCK // SUBLANES

GLA_LEVELS = (32, 16, 8, 4, 2, 1)
GLA_SPLIT = 3


def _cparams(*sem):
    return pltpu.CompilerParams(dimension_semantics=sem, vmem_limit_bytes=VMEM_LIMIT)


def _const_spec(shape):
    nd = len(shape)
    return pl.BlockSpec(shape, lambda *_: (0,) * nd, pipeline_mode=pl.Buffered(1))


def _rms(xf, g):
    return xf * lax.rsqrt(jnp.mean(xf * xf, axis=-1, keepdims=True) + EPS) * g


def _sigmoid(x):
    return 1.0 / (1.0 + jnp.exp(-x))


def _store_lane_chunks(ref, value):
    for c in range(ref.shape[0]):
        ref[c] = value[:, c * LANES:(c + 1) * LANES]


def _load_strided_rows(ref, first, stride):
    return jnp.concatenate([ref[c, pl.ds(first, SUBLANES, stride=stride), :]
                            for c in range(ref.shape[0])], axis=1)


STAGE_ROWS = 256


def _fetch_as_bf16(w_hbm, w_bf, stage, sem):
    rows = stage.shape[1]
    n_chunks = w_hbm.shape[0] // rows

    def copy(k):
        return pltpu.make_async_copy(w_hbm.at[pl.ds(k * rows, rows), :], stage.at[k % 2],
                                     sem.at[k % 2])

    copy(0).start()
    for k in range(n_chunks):
        if k + 1 < n_chunks:
            copy(k + 1).start()
        copy(k).wait()
        w_bf[k * rows:(k + 1) * rows, :] = stage[k % 2].astype(BF16)


def _resident_weight_scratch(*weights):
    cols = weights[0].shape[1]
    assert all(w.shape[1] == cols and w.shape[0] % STAGE_ROWS == 0 for w in weights)
    return ([pltpu.VMEM(w.shape, BF16) for w in weights]
            + [pltpu.VMEM((2, STAGE_ROWS, cols), F32), pltpu.SemaphoreType.DMA((2,))])


def _dot(a, b):
    return jnp.dot(a, b, preferred_element_type=F32)


def _dot_nt(a, b):
    return lax.dot_general(a, b, (((1,), (1,)), ((), ())), preferred_element_type=F32)


def _dot_tn(a, b):
    return lax.dot_general(a, b, (((0,), (0,)), ((), ())), preferred_element_type=F32)


def _norm_gate_kernel(x_ref, g_ref, wa1_ref, wa2_ref, ba_ref, h_ref, la_ref):
    h = _rms(x_ref[...], g_ref[...]).astype(BF16)
    h_ref[...] = h
    a = _dot_nt(h, wa1_ref[...].astype(BF16))
    gp = _dot(a.astype(BF16), wa2_ref[...]) + ba_ref[...]
    ls = jnp.minimum(gp, 0.0) - jnp.log1p(jnp.exp(-jnp.abs(gp)))
    la_ref[...] = ls * (1.0 / GLA_GATE_NORM)


def _norm_gate(x2d, g, w_in_t, gate_row0, wa2, ba, tm=1024):
    n, d = x2d.shape
    dk = wa2.shape[1]
    return pl.pallas_call(
        _norm_gate_kernel,
        grid=(n // tm,),
        in_specs=[
            pl.BlockSpec((tm, d), lambda i: (i, 0)),
            _const_spec((1, d)),
            pl.BlockSpec((LANES, d), lambda i: (gate_row0 // LANES, 0), pipeline_mode=pl.Buffered(1)),
            _const_spec(wa2.shape),
            _const_spec((1, dk)),
        ],
        out_specs=[
            pl.BlockSpec((tm, d), lambda i: (i, 0)),
            pl.BlockSpec((tm, dk), lambda i: (i, 0)),
        ],
        out_shape=[
            jax.ShapeDtypeStruct((n, d), BF16),
            jax.ShapeDtypeStruct((n, dk), F32),
        ],
        compiler_params=_cparams("parallel"),
        name="norm_gate",
    )(x2d, g, w_in_t, wa2, ba)


def _mm_nt_kernel(a_ref, wt_ref, o_ref, wbf_ref):
    @pl.when(pl.program_id(1) == 0)
    def _():
        wbf_ref[...] = wt_ref[...].astype(BF16)

    o_ref[...] = _dot_nt(a_ref[...], wbf_ref[...]).astype(o_ref.dtype)


def _matmul_nt(a, wt, row0, m, tm, tn, out_dtype, name):
    n, k = a.shape
    return pl.pallas_call(
        _mm_nt_kernel,
        grid=(m // tn, n // tm),
        in_specs=[
            pl.BlockSpec((tm, k), lambda j, i: (i, 0)),
            pl.BlockSpec((pl.Element(tn), pl.Element(k)),
                         lambda j, i: (pl.multiple_of(row0 + j * tn, SUBLANES), 0)),
        ],
        out_specs=pl.BlockSpec((tm, tn), lambda j, i: (i, j)),
        out_shape=jax.ShapeDtypeStruct((n, m), out_dtype),
        scratch_shapes=[pltpu.VMEM((tn, k), BF16)],
        compiler_params=_cparams("parallel", "arbitrary"),
        name=name,
    )(a, wt)


def _gla_consts():
    c = GLA_CHUNK
    i = np.arange(c)[:, None]
    t = np.arange(c)[None, :]
    blocks = [t <= i, t > i]
    masks = []
    for s in GLA_LEVELS:
        upper = (i % (2 * s)) >= s
        blk_start = i - (i % s)
        blk_end = blk_start + s - 1
        blocks.append((upper & (t >= blk_start) & (t <= i)) | (~upper & (t > i) & (t <= blk_end)))
        same_group = (i // (2 * s)) == (t // (2 * s))
        masks.append(same_group & upper & ((t % (2 * s)) < s))
    masks.append(i == t)
    assert (np.sum(masks, axis=0) == (t <= i)).all()
    expo = np.concatenate(blocks, axis=0).astype(np.float32)
    expo = np.concatenate([expo] * GLA_SPLIT, axis=1)
    return jnp.asarray(expo, BF16), jnp.asarray(np.stack(masks).astype(np.float32))


def _gla_kernel(q_ref, k_ref, v_ref, r_ref, la_ref, gg_ref, expo_ref, mask_ref, o_ref, st_ref,
                att_ref, qin_ref, kout_ref, elast_ref, dec_ref, *, q_scale):
    c = GLA_CHUNK
    n_levels = len(GLA_LEVELS)
    hk = q_ref.shape[1] // GLA_HEADS
    hv = v_ref.shape[1] // GLA_HEADS

    @pl.when(pl.program_id(1) == 0)
    def _():
        st_ref[...] = jnp.zeros_like(st_ref)

    def decays(ci, carry):
        r0 = pl.multiple_of(ci * c, c)
        rows = pl.ds(r0, c)
        g = la_ref[rows, :] * LOG2_E
        g1 = g.astype(BF16)
        rem = g - g1.astype(F32)
        g2 = rem.astype(BF16)
        g3 = (rem - g2.astype(F32)).astype(BF16)
        dec_ref[ci] = jnp.exp2(_dot(expo_ref[...], jnp.concatenate([g1, g2, g3], axis=0)))
        return carry

    def chunk(ci, carry):
        r0 = pl.multiple_of(ci * c, c)
        rows = pl.ds(r0, c)
        for hh in range(GLA_HEADS):
            kc = slice(hh * hk, (hh + 1) * hk)
            dec = dec_ref[ci, :, kc]

            q = q_ref[rows, kc].astype(F32) * q_scale
            k = k_ref[rows, kc].astype(F32)

            att = mask_ref[n_levels] * jnp.sum(q * k, axis=-1, keepdims=True)
            for l in range(n_levels):
                f = dec[(2 + l) * c:(3 + l) * c]
                att = att + mask_ref[l] * _dot_nt((q * f).astype(BF16), (k * f).astype(BF16))

            att_ref[ci, hh] = att.astype(BF16)
            qin_ref[rows, kc] = (q * dec[0:c]).astype(BF16)
            kout_ref[rows, kc] = (k * dec[c:2 * c]).astype(BF16)
            elast_ref[ci, hh] = dec[c - SUBLANES:c]
        return carry

    def scan(ci, carry):
        r0 = pl.multiple_of(ci * c, c)
        rows = pl.ds(r0, c)
        for hh in range(GLA_HEADS):
            kc = slice(hh * hk, (hh + 1) * hk)
            vc = slice(hh * hv, (hh + 1) * hv)
            v = v_ref[rows, vc]
            st = st_ref[hh]
            o = _dot(att_ref[ci, hh], v) + _dot_nt(qin_ref[rows, kc], st.astype(BF16))
            st_ref[hh] = (st * elast_ref[ci, hh][SUBLANES - 1:SUBLANES]
                          + _dot_tn(v, kout_ref[rows, kc]))

            on = o * lax.rsqrt(jnp.mean(o * o, axis=-1, keepdims=True) + EPS) * gg_ref[:, vc]
            r = r_ref[rows, vc].astype(F32)
            o_ref[rows, vc] = (on * (r * _sigmoid(r))).astype(o_ref.dtype)
        return carry

    n_chunks = q_ref.shape[0] // c
    lax.fori_loop(0, n_chunks, decays, 0, unroll=8)
    lax.fori_loop(0, n_chunks, chunk, 0, unroll=8)
    lax.fori_loop(0, n_chunks, scan, 0, unroll=8)


def _gla(proj, log_a, g_gla, batch, seq, dk, dv, tt=1024):
    n = proj.shape[0]
    hk = dk // GLA_HEADS
    hv = dv // GLA_HEADS
    expo, masks = _gla_consts()
    spb = seq // tt
    v_blk = 2 * dk // dv
    return pl.pallas_call(
        functools.partial(_gla_kernel, q_scale=hk ** -0.5),
        grid=(batch, spb),
        in_specs=[
            pl.BlockSpec((tt, dk), lambda b, s: (b * spb + s, 0)),
            pl.BlockSpec((tt, dk), lambda b, s: (b * spb + s, 1)),
            pl.BlockSpec((tt, dv), lambda b, s: (b * spb + s, v_blk)),
            pl.BlockSpec((tt, dv), lambda b, s: (b * spb + s, v_blk + 1)),
            pl.BlockSpec((tt, dk), lambda b, s: (b * spb + s, 0)),
            _const_spec((1, dv)),
            _const_spec(expo.shape),
            _const_spec(masks.shape),
        ],
        out_specs=pl.BlockSpec((tt, dv), lambda b, s: (b * spb + s, 0)),
        out_shape=jax.ShapeDtypeStruct((n, dv), BF16),
        scratch_shapes=[
            pltpu.VMEM((GLA_HEADS, hv, hk), F32),
            pltpu.VMEM((tt // GLA_CHUNK, GLA_HEADS, GLA_CHUNK, GLA_CHUNK), BF16),
            pltpu.VMEM((tt, dk), BF16),
            pltpu.VMEM((tt, dk), BF16),
            pltpu.VMEM((tt // GLA_CHUNK, GLA_HEADS, SUBLANES, hk), F32),
            pltpu.VMEM((tt // GLA_CHUNK, expo.shape[0], dk), F32),
        ],
        compiler_params=_cparams("parallel", "arbitrary"),
        name="gla",
    )(proj, proj, proj, proj, log_a, g_gla, expo, masks)


def _pool_kernel(p_ref, w_ref, sc_ref, o_ref, *, seq, gw):
    row = lax.broadcasted_iota(jnp.int32, (seq, gw), 0)
    row1 = lax.broadcasted_iota(jnp.int32, (seq, 1), 0)
    for gi, w in enumerate(POOL_WINDOWS):
        cols = slice(gi * gw, (gi + 1) * gw)
        x = p_ref[:, cols].astype(F32)
        acc = x
        d = 1
        while d < w:
            acc = acc + jnp.where(row >= d, pltpu.roll(acc, d, 0), 0.0)
            d *= 2
        inv_cnt = 1.0 / jnp.minimum(row1 + 1, w).astype(F32)
        pooled = acc * inv_cnt - x
        mixed = _dot(pooled.astype(BF16), w_ref[gi]) * sc_ref[:, cols]
        o_ref[:, cols] = mixed.astype(o_ref.dtype)


def _pool(proj, w_pool, pool_scale, batch, seq):
    n = proj.shape[0]
    ng, gw, _ = w_pool.shape
    width = ng * gw
    return pl.pallas_call(
        functools.partial(_pool_kernel, seq=seq, gw=gw),
        grid=(batch,),
        in_specs=[
            pl.BlockSpec((seq, width), lambda b: (b, 0)),
            _const_spec(w_pool.shape),
            _const_spec((1, width)),
        ],
        out_specs=pl.BlockSpec((seq, width), lambda b: (b, 0)),
        out_shape=jax.ShapeDtypeStruct((n, width), BF16),
        compiler_params=_cparams("parallel"),
        name="pool",
    )(proj, w_pool, pool_scale)


def _merge_kernel(og_ref, op_ref, pg_ref, x_ref, wb_hbm, wo_hbm, gn_ref, x1_ref, h_ref,
                  wb_ref, wo_ref, stage, sem):
    @pl.when(pl.program_id(0) == 0)
    def _():
        _fetch_as_bf16(wb_hbm, wb_ref, stage, sem)
        _fetch_as_bf16(wo_hbm, wo_ref, stage, sem)

    dv = og_ref.shape[1]
    d = x_ref.shape[1]
    g0 = pg_ref.shape[1] - 2 * d
    yg = _dot(og_ref[...], wb_ref[0:dv, :])
    yp = _dot(op_ref[...], wb_ref[dv:, :])
    merged = (_sigmoid(pg_ref[:, g0:g0 + d].astype(F32)) * yg
              + _sigmoid(pg_ref[:, g0 + d:].astype(F32)) * yp)
    x1 = x_ref[...] + _dot(merged.astype(BF16), wo_ref[...])
    x1_ref[...] = x1
    h_ref[...] = _rms(x1, gn_ref[...]).astype(h_ref.dtype)


def _merge(o_gla, o_pool, proj_pg, x2d, w_branch, w_out, g_next, tm=256):
    n, d = x2d.shape
    dv = o_gla.shape[1]
    dp = o_pool.shape[1]
    wpg = proj_pg.shape[1]
    return pl.pallas_call(
        _merge_kernel,
        grid=(n // tm,),
        in_specs=[
            pl.BlockSpec((tm, dv), lambda i: (i, 0)),
            pl.BlockSpec((tm, dp), lambda i: (i, 0)),
            pl.BlockSpec((tm, wpg), lambda i: (i, 0)),
            pl.BlockSpec((tm, d), lambda i: (i, 0)),
            pl.BlockSpec(memory_space=pl.ANY),
            pl.BlockSpec(memory_space=pl.ANY),
            _const_spec((1, d)),
        ],
        out_specs=[
            pl.BlockSpec((tm, d), lambda i: (i, 0)),
            pl.BlockSpec((tm, d), lambda i: (i, 0)),
        ],
        out_shape=[
            jax.ShapeDtypeStruct((n, d), F32),
            jax.ShapeDtypeStruct((n, d), BF16),
        ],
        scratch_shapes=_resident_weight_scratch(w_branch, w_out),
        compiler_params=_cparams("arbitrary"),
        name="merge",
    )(o_gla, o_pool, proj_pg, x2d, w_branch, w_out, g_next)


def _norm_mm_kernel(x_ref, g_ref, w_ref, o_ref, wbf_ref):
    @pl.when(pl.program_id(1) == 0)
    def _():
        wbf_ref[...] = w_ref[...].astype(BF16)

    h = _rms(x_ref[...], g_ref[...]).astype(BF16)
    o_ref[...] = _dot(h, wbf_ref[...]).astype(o_ref.dtype)


def _norm_matmul(x2d, g, w, tm, tn, name):
    n, d = x2d.shape
    m = w.shape[1]
    return pl.pallas_call(
        _norm_mm_kernel,
        grid=(m // tn, n // tm),
        in_specs=[
            pl.BlockSpec((tm, d), lambda j, i: (i, 0)),
            _const_spec((1, d)),
            pl.BlockSpec((d, tn), lambda j, i: (0, j)),
        ],
        out_specs=pl.BlockSpec((tm, tn), lambda j, i: (i, j)),
        out_shape=jax.ShapeDtypeStruct((n, m), BF16),
        scratch_shapes=[pltpu.VMEM((d, tn), BF16)],
        compiler_params=_cparams("parallel", "arbitrary"),
        name=name,
    )(x2d, g, w)


def _cross_fold_kernel(kv_ref, wq_hbm, wo_hbm, wqk_ref, vw_ref, wq_ref, wo_ref, stage, sem):
    @pl.when(pl.program_id(0) == 0)
    def _():
        _fetch_as_bf16(wq_hbm, wq_ref, stage, sem)
        _fetch_as_bf16(wo_hbm, wo_ref, stage, sem)

    d = wq_ref.shape[0]
    n_mem = kv_ref.shape[0]
    hd = d // CROSS_HEADS
    for hh in range(CROSS_HEADS):
        cols = slice(hh * hd, (hh + 1) * hd)
        wqk = _dot_nt(wq_ref[:, cols], kv_ref[:, cols]) * (hd ** -0.5)
        wqk_ref[:, hh * n_mem:(hh + 1) * n_mem] = wqk.astype(wqk_ref.dtype)
        vw = _dot(kv_ref[:, d + hh * hd:d + (hh + 1) * hd], wo_ref[cols, :])
        vw_ref[hh * n_mem:(hh + 1) * n_mem, :] = vw.astype(vw_ref.dtype)


def _cross_fold(kv, w_cq, w_co, batch, n_mem):
    d = w_cq.shape[0]
    hm = CROSS_HEADS * n_mem
    return pl.pallas_call(
        _cross_fold_kernel,
        grid=(batch,),
        in_specs=[
            pl.BlockSpec((n_mem, 2 * d), lambda b: (b, 0)),
            pl.BlockSpec(memory_space=pl.ANY),
            pl.BlockSpec(memory_space=pl.ANY),
        ],
        out_specs=[
            pl.BlockSpec((d, hm), lambda b: (b, 0)),
            pl.BlockSpec((hm, d), lambda b: (b, 0)),
        ],
        out_shape=[
            jax.ShapeDtypeStruct((batch * d, hm), BF16),
            jax.ShapeDtypeStruct((batch * hm, d), BF16),
        ],
        scratch_shapes=_resident_weight_scratch(w_cq, w_co),
        compiler_params=_cparams("arbitrary"),
        name="cross_fold",
    )(kv, w_cq, w_co)


def _cross_kernel(h_ref, wqk_ref, vw_ref, x_ref, gn_ref, x2_ref, hn_ref, perm_ref):
    tm, d = h_ref.shape
    assert tm == FFN_BLOCK
    n_mem = wqk_ref.shape[1] // CROSS_HEADS
    s_all = _dot(h_ref[...], wqk_ref[...])
    probs = []
    for hh in range(CROSS_HEADS):
        s = s_all[:, hh * n_mem:(hh + 1) * n_mem]
        p = jnp.exp(s - jnp.max(s, axis=-1, keepdims=True))
        probs.append((p * (1.0 / jnp.sum(p, axis=-1, keepdims=True))).astype(BF16))
    x2 = x_ref[...] + _dot(jnp.concatenate(probs, axis=1), vw_ref[...])
    x2_ref[...] = x2
    hn = _rms(x2, gn_ref[...])
    for k in range(FFN_GROUPS):
        t0 = k * SUBLANES
        first = SUBLANES * (t0 % FFN_GROUPS) + t0 // FFN_GROUPS
        for c in range(perm_ref.shape[0]):
            perm_ref[c, pl.ds(first, SUBLANES, stride=SUBLANES), :] = (
                hn[t0:t0 + SUBLANES, c * LANES:(c + 1) * LANES])
    hn_ref[...] = jnp.concatenate([perm_ref[c] for c in range(perm_ref.shape[0])],
                                  axis=1).astype(hn_ref.dtype)


def _cross(h, wqk, vw, x2d, g_next, batch, seq, tm=FFN_BLOCK):
    n, d = x2d.shape
    hm = wqk.shape[1]
    spb = seq // tm
    return pl.pallas_call(
        _cross_kernel,
        grid=(batch, spb),
        in_specs=[
            pl.BlockSpec((tm, d), lambda b, i: (b * spb + i, 0)),
            pl.BlockSpec((d, hm), lambda b, i: (b, 0)),
            pl.BlockSpec((hm, d), lambda b, i: (b, 0)),
            pl.BlockSpec((tm, d), lambda b, i: (b * spb + i, 0)),
            _const_spec((1, d)),
        ],
        out_specs=[
            pl.BlockSpec((tm, d), lambda b, i: (b * spb + i, 0)),
            pl.BlockSpec((tm, d), lambda b, i: (b * spb + i, 0)),
        ],
        out_shape=[
            jax.ShapeDtypeStruct((n, d), F32),
            jax.ShapeDtypeStruct((n, d), BF16),
        ],
        scratch_shapes=[pltpu.VMEM((d // LANES, tm, LANES), F32)],
        compiler_params=_cparams("parallel", "parallel"),
        name="cross",
    )(h, wqk, vw, x2d, g_next)


def _ffn_up_kernel(h_ref, wg32_ref, wv32_ref, cwg_ref, cwv_ref, cbg_ref, cbv_ref, act_ref,
                   wg_ref, wv_ref, *, rb):
    seq, tn = act_ref.shape
    row = lax.broadcasted_iota(jnp.int32, (SUBLANES, tn), 0)

    @pl.when(pl.program_id(1) == 0)
    def _():
        wg_ref[...] = wg32_ref[...].astype(BF16)
        wv_ref[...] = wv32_ref[...].astype(BF16)

    def wrap(group, prev_group):
        return jnp.where(row == 0, prev_group[SUBLANES - 1:SUBLANES], pltpu.roll(group, 1, 0))

    def conv(u, prev, cw_ref, cb_ref):
        w30 = wrap(u[rb - 2 * SUBLANES:rb - SUBLANES], prev[:SUBLANES])
        w31 = wrap(u[rb - SUBLANES:], prev[SUBLANES:])
        u1 = jnp.concatenate([w31, u[:rb - SUBLANES]], axis=0)
        u2 = jnp.concatenate([w30, w31, u[:rb - 2 * SUBLANES]], axis=0)
        return cw_ref[2:3] * u + cw_ref[1:2] * u1 + cw_ref[0:1] * u2 + cb_ref[...]

    prev_g = jnp.zeros((2 * SUBLANES, tn), F32)
    prev_v = jnp.zeros((2 * SUBLANES, tn), F32)
    for r0 in range(0, seq, rb):
        hb = h_ref[r0:r0 + rb, :]
        ug = _dot(hb, wg_ref[...])
        uv = _dot(hb, wv_ref[...])
        gate = conv(ug, prev_g, cwg_ref, cbg_ref)
        val = conv(uv, prev_v, cwv_ref, cbv_ref)
        prev_g = ug[rb - 2 * SUBLANES:]
        prev_v = uv[rb - 2 * SUBLANES:]
        act_ref[r0:r0 + rb, :] = (gate * _sigmoid(gate) * val).astype(act_ref.dtype)


def _ffn_up(h, w_up, conv_w, conv_b, batch, seq, tn=512, rb=FFN_BLOCK):
    n, d = h.shape
    dff = w_up.shape[1] // 2
    nj = dff // tn
    return pl.pallas_call(
        functools.partial(_ffn_up_kernel, rb=rb),
        grid=(nj, batch),
        in_specs=[
            pl.BlockSpec((seq, d), lambda j, b: (b, 0)),
            pl.BlockSpec((d, tn), lambda j, b: (0, j)),
            pl.BlockSpec((d, tn), lambda j, b: (0, nj + j)),
            pl.BlockSpec((CONV_W, tn), lambda j, b: (0, j)),
            pl.BlockSpec((CONV_W, tn), lambda j, b: (0, nj + j)),
            pl.BlockSpec((1, tn), lambda j, b: (0, j)),
            pl.BlockSpec((1, tn), lambda j, b: (0, nj + j)),
        ],
        out_specs=pl.BlockSpec((seq, tn), lambda j, b: (b, j)),
        out_shape=jax.ShapeDtypeStruct((n, dff), BF16),
        scratch_shapes=[pltpu.VMEM((d, tn), BF16), pltpu.VMEM((d, tn), BF16)],
        compiler_params=_cparams("parallel", "arbitrary"),
        name="ffn_up",
    )(h, w_up, w_up, conv_w, conv_w, conv_b, conv_b)


def _ffn_down_kernel(a_ref, w_hbm, x_ref, g_ref, o_ref, y_ref, w_ref, stage, sem):
    assert a_ref.shape[0] == FFN_BLOCK

    @pl.when(pl.program_id(0) == 0)
    def _():
        _fetch_as_bf16(w_hbm, w_ref, stage, sem)

    _store_lane_chunks(y_ref, _dot(a_ref[...], w_ref[...]))
    for k in range(FFN_GROUPS):
        t0 = k * SUBLANES
        first = SUBLANES * (t0 % FFN_GROUPS) + t0 // FFN_GROUPS
        y = _load_strided_rows(y_ref, first, SUBLANES)
        rows = slice(t0, t0 + SUBLANES)
        o_ref[rows, :] = _rms(x_ref[rows, :] + y, g_ref[...])


def _ffn_down(act, w_down, x2d, g, tm=FFN_BLOCK):
    n, d = x2d.shape
    dff = act.shape[1]
    return pl.pallas_call(
        _ffn_down_kernel,
        grid=(n // tm,),
        in_specs=[
            pl.BlockSpec((tm, dff), lambda i: (i, 0)),
            pl.BlockSpec(memory_space=pl.ANY),
            pl.BlockSpec((tm, d), lambda i: (i, 0)),
            _const_spec((1, d)),
        ],
        out_specs=pl.BlockSpec((tm, d), lambda i: (i, 0)),
        out_shape=jax.ShapeDtypeStruct((n, d), F32),
        scratch_shapes=([pltpu.VMEM((d // LANES, tm, LANES), F32)]
                        + _resident_weight_scratch(w_down)),
        compiler_params=_cparams("arbitrary"),
        name="ffn_down",
    )(act, w_down, x2d, g)


def kernel(x, mem, g_mix, w_in, w_a2, b_a, g_gla, w_pool, pool_scale, w_branch, w_out,
           g_cross, g_mem, w_cq, w_ckv, w_co, g_ffn, w_up, conv_w, conv_b, w_down, g_final):
    batch, seq, d = x.shape
    n_mem = mem.shape[1]
    depth = w_in.shape[0]
    assert depth == 1
    dk = w_a2.shape[2]
    dv = g_gla.shape[1]
    off_a = 2 * dk + 2 * dv
    off_p = off_a + GLA_GATE_RANK
    row2 = lambda v: v.reshape(1, -1)

    xs = x.reshape(batch * seq, d)
    mem2d = mem.reshape(batch * n_mem, d)

    w_in_t = jnp.swapaxes(w_in[0], 0, 1)
    d_in = w_in_t.shape[0]
    wa2 = jnp.pad(w_a2[0], ((0, LANES - GLA_GATE_RANK), (0, 0))).astype(BF16)

    h, log_a = _norm_gate(xs, row2(g_mix[0]), w_in_t, off_a, wa2, row2(b_a[0]))
    proj_qkvr = _matmul_nt(h, w_in_t, 0, off_a, 1024, 1024, BF16, "in_proj_qkvr")
    proj_pg = _matmul_nt(h, w_in_t, off_p, d_in - off_p, 1024, 1024, BF16, "in_proj_pg")
    o_gla = _gla(proj_qkvr, log_a, row2(g_gla[0]), batch, seq, dk, dv)
    o_pool = _pool(proj_pg, w_pool[0].astype(BF16), row2(pool_scale[0]), batch, seq)
    x1, h2 = _merge(o_gla, o_pool, proj_pg, xs, w_branch[0], w_out[0],
                    row2(g_cross[0]))
    kv = _norm_matmul(mem2d, row2(g_mem[0]), w_ckv[0], 1024, 1024, "mem_kv")
    wqk, vw = _cross_fold(kv, w_cq[0], w_co[0], batch, n_mem)
    x2, h3 = _cross(h2, wqk, vw, x1, row2(g_ffn[0]), batch, seq)
    act = _ffn_up(h3, w_up[0], conv_w[0], row2(conv_b[0]), batch, seq)
    out = _ffn_down(act, w_down[0], x2, row2(g_final))
    return out.reshape(batch, seq, d)
```

```python
import functools

import numpy as np
import jax
import jax.numpy as jnp
from jax import lax
from jax.experimental import pallas as pl
from jax.experimental.pallas import tpu as pltpu

F32 = jnp.float32
BF16 = jnp.bfloat16

EPS = 1e-6
LOG2_E = 1.4426950408889634
GLA_HEADS = 4
GLA_GATE_RANK = 16
GLA_GATE_NORM = 16.0
GLA_CHUNK = 64
POOL_WINDOWS = (2, 4, 8, 16)
CROSS_HEADS = 4
CONV_W = 3

LANES = 128
SUBLANES = 8
VMEM_LIMIT = 52 * 1024 * 1024

FFN_BLOCK = 256
FFN_GROUPS = FFN_BLOCK // SUBLANES

GLA_LEVELS = (32, 16, 8, 4, 2, 1)
GLA_SPLIT = 3


def _cparams(*sem):
    return pltpu.CompilerParams(dimension_semantics=sem, vmem_limit_bytes=VMEM_LIMIT)


def _const_spec(shape):
    nd = len(shape)
    return pl.BlockSpec(shape, lambda *_: (0,) * nd, pipeline_mode=pl.Buffered(1))


def _rms(xf, g):
    return xf * lax.rsqrt(jnp.mean(xf * xf, axis=-1, keepdims=True) + EPS) * g


def _sigmoid(x):
    return 1.0 / (1.0 + jnp.exp(-x))


def _store_lane_chunks(ref, value):
    for c in range(ref.shape[0]):
        ref[c] = value[:, c * LANES:(c + 1) * LANES]


def _load_strided_rows(ref, first, stride):
    return jnp.concatenate([ref[c, pl.ds(first, SUBLANES, stride=stride), :]
                            for c in range(ref.shape[0])], axis=1)


STAGE_ROWS = 256


def _fetch_as_bf16(w_hbm, w_bf, stage, sem):
    rows = stage.shape[1]
    n_chunks = w_hbm.shape[0] // rows

    def copy(k):
        return pltpu.make_async_copy(w_hbm.at[pl.ds(k * rows, rows), :], stage.at[k % 2],
                                     sem.at[k % 2])

    copy(0).start()
    for k in range(n_chunks):
        if k + 1 < n_chunks:
            copy(k + 1).start()
        copy(k).wait()
        w_bf[k * rows:(k + 1) * rows, :] = stage[k % 2].astype(BF16)


def _resident_weight_scratch(*weights):
    cols = weights[0].shape[1]
    assert all(w.shape[1] == cols and w.shape[0] % STAGE_ROWS == 0 for w in weights)
    return ([pltpu.VMEM(w.shape, BF16) for w in weights]
            + [pltpu.VMEM((2, STAGE_ROWS, cols), F32), pltpu.SemaphoreType.DMA((2,))])


def _dot(a, b):
    return jnp.dot(a, b, preferred_element_type=F32)


def _dot_nt(a, b):
    return lax.dot_general(a, b, (((1,), (1,)), ((), ())), preferred_element_type=F32)


def _dot_tn(a, b):
    return lax.dot_general(a, b, (((0,), (0,)), ((), ())), preferred_element_type=F32)


def _norm_gate_kernel(x_ref, g_ref, wa1_ref, wa2_ref, ba_ref, h_ref, la_ref):
    h = _rms(x_ref[...], g_ref[...]).astype(BF16)
    h_ref[...] = h
    a = _dot_nt(h, wa1_ref[...].astype(BF16))
    gp = _dot(a.astype(BF16), wa2_ref[...]) + ba_ref[...]
    ls = jnp.minimum(gp, 0.0) - jnp.log1p(jnp.exp(-jnp.abs(gp)))
    la_ref[...] = ls * (1.0 / GLA_GATE_NORM)


def _norm_gate(x2d, g, w_in_t, gate_row0, wa2, ba, tm=1024):
    n, d = x2d.shape
    dk = wa2.shape[1]
    return pl.pallas_call(
        _norm_gate_kernel,
        grid=(n // tm,),
        in_specs=[
            pl.BlockSpec((tm, d), lambda i: (i, 0)),
            _const_spec((1, d)),
            pl.BlockSpec((LANES, d), lambda i: (gate_row0 // LANES, 0), pipeline_mode=pl.Buffered(1)),
            _const_spec(wa2.shape),
            _const_spec((1, dk)),
        ],
        out_specs=[
            pl.BlockSpec((tm, d), lambda i: (i, 0)),
            pl.BlockSpec((tm, dk), lambda i: (i, 0)),
        ],
        out_shape=[
            jax.ShapeDtypeStruct((n, d), BF16),
            jax.ShapeDtypeStruct((n, dk), F32),
        ],
        compiler_params=_cparams("parallel"),
        name="norm_gate",
    )(x2d, g, w_in_t, wa2, ba)


def _mm_nt_kernel(a_ref, wt_ref, o_ref, wbf_ref):
    @pl.when(pl.program_id(1) == 0)
    def _():
        wbf_ref[...] = wt_ref[...].astype(BF16)

    o_ref[...] = _dot_nt(a_ref[...], wbf_ref[...]).astype(o_ref.dtype)


def _matmul_nt(a, wt, row0, m, tm, tn, out_dtype, name):
    n, k = a.shape
    return pl.pallas_call(
        _mm_nt_kernel,
        grid=(m // tn, n // tm),
        in_specs=[
            pl.BlockSpec((tm, k), lambda j, i: (i, 0)),
            pl.BlockSpec((pl.Element(tn), pl.Element(k)),
                         lambda j, i: (pl.multiple_of(row0 + j * tn, SUBLANES), 0)),
        ],
        out_specs=pl.BlockSpec((tm, tn), lambda j, i: (i, j)),
        out_shape=jax.ShapeDtypeStruct((n, m), out_dtype),
        scratch_shapes=[pltpu.VMEM((tn, k), BF16)],
        compiler_params=_cparams("parallel", "arbitrary"),
        name=name,
    )(a, wt)


def _gla_consts():
    c = GLA_CHUNK
    i = np.arange(c)[:, None]
    t = np.arange(c)[None, :]
    blocks = [t <= i, t > i]
    masks = []
    for s in GLA_LEVELS:
        upper = (i % (2 * s)) >= s
        blk_start = i - (i % s)
        blk_end = blk_start + s - 1
        blocks.append((upper & (t >= blk_start) & (t <= i)) | (~upper & (t > i) & (t <= blk_end)))
        same_group = (i // (2 * s)) == (t // (2 * s))
        masks.append(same_group & upper & ((t % (2 * s)) < s))
    masks.append(i == t)
    assert (np.sum(masks, axis=0) == (t <= i)).all()
    expo = np.concatenate(blocks, axis=0).astype(np.float32)
    expo = np.concatenate([expo] * GLA_SPLIT, axis=1)
    return jnp.asarray(expo, BF16), jnp.asarray(np.stack(masks).astype(np.float32))


def _gla_kernel(q_ref, k_ref, v_ref, r_ref, la_ref, gg_ref, expo_ref, mask_ref, o_ref, st_ref,
                att_ref, qin_ref, kout_ref, elast_ref, dec_ref, *, q_scale):
    c = GLA_CHUNK
    n_levels = len(GLA_LEVELS)
    hk = q_ref.shape[1] // GLA_HEADS
    hv = v_ref.shape[1] // GLA_HEADS

    @pl.when(pl.program_id(1) == 0)
    def _():
        st_ref[...] = jnp.zeros_like(st_ref)

    def decays(ci, carry):
        r0 = pl.multiple_of(ci * c, c)
        rows = pl.ds(r0, c)
        g = la_ref[rows, :] * LOG2_E
        g1 = g.astype(BF16)
        rem = g - g1.astype(F32)
        g2 = rem.astype(BF16)
        g3 = (rem - g2.astype(F32)).astype(BF16)
        dec_ref[ci] = jnp.exp2(_dot(expo_ref[...], jnp.concatenate([g1, g2, g3], axis=0)))
        return carry

    def chunk(ci, carry):
        r0 = pl.multiple_of(ci * c, c)
        rows = pl.ds(r0, c)
        for hh in range(GLA_HEADS):
            kc = slice(hh * hk, (hh + 1) * hk)
            dec = dec_ref[ci, :, kc]

            q = q_ref[rows, kc].astype(F32) * q_scale
            k = k_ref[rows, kc].astype(F32)

            att = mask_ref[n_levels] * jnp.sum(q * k, axis=-1, keepdims=True)
            for l in range(n_levels):
                f = dec[(2 + l) * c:(3 + l) * c]
                att = att + mask_ref[l] * _dot_nt((q * f).astype(BF16), (k * f).astype(BF16))

            att_ref[ci, hh] = att.astype(BF16)
            qin_ref[rows, kc] = (q * dec[0:c]).astype(BF16)
            kout_ref[rows, kc] = (k * dec[c:2 * c]).astype(BF16)
            elast_ref[ci, hh] = dec[c - SUBLANES:c]
        return carry

    def scan(ci, carry):
        r0 = pl.multiple_of(ci * c, c)
        rows = pl.ds(r0, c)
        for hh in range(GLA_HEADS):
            kc = slice(hh * hk, (hh + 1) * hk)
            vc = slice(hh * hv, (hh + 1) * hv)
            v = v_ref[rows, vc]
            st = st_ref[hh]
            o = _dot(att_ref[ci, hh], v) + _dot_nt(qin_ref[rows, kc], st.astype(BF16))
            st_ref[hh] = (st * elast_ref[ci, hh][SUBLANES - 1:SUBLANES]
                          + _dot_tn(v, kout_ref[rows, kc]))

            on = o * lax.rsqrt(jnp.mean(o * o, axis=-1, keepdims=True) + EPS) * gg_ref[:, vc]
            r = r_ref[rows, vc].astype(F32)
            o_ref[rows, vc] = (on * (r * _sigmoid(r))).astype(o_ref.dtype)
        return carry

    n_chunks = q_ref.shape[0] // c
    lax.fori_loop(0, n_chunks, decays, 0, unroll=8)
    lax.fori_loop(0, n_chunks, chunk, 0, unroll=16)
    lax.fori_loop(0, n_chunks, scan, 0, unroll=16)


def _gla(proj, log_a, g_gla, batch, seq, dk, dv, tt=1024):
    n = proj.shape[0]
    hk = dk // GLA_HEADS
    hv = dv // GLA_HEADS
    expo, masks = _gla_consts()
    spb = seq // tt
    v_blk = 2 * dk // dv
    return pl.pallas_call(
        functools.partial(_gla_kernel, q_scale=hk ** -0.5),
        grid=(batch, spb),
        in_specs=[
            pl.BlockSpec((tt, dk), lambda b, s: (b * spb + s, 0)),
            pl.BlockSpec((tt, dk), lambda b, s: (b * spb + s, 1)),
            pl.BlockSpec((tt, dv), lambda b, s: (b * spb + s, v_blk)),
            pl.BlockSpec((tt, dv), lambda b, s: (b * spb + s, v_blk + 1)),
            pl.BlockSpec((tt, dk), lambda b, s: (b * spb + s, 0)),
            _const_spec((1, dv)),
            _const_spec(expo.shape),
            _const_spec(masks.shape),
        ],
        out_specs=pl.BlockSpec((tt, dv), lambda b, s: (b * spb + s, 0)),
        out_shape=jax.ShapeDtypeStruct((n, dv), BF16),
        scratch_shapes=[
            pltpu.VMEM((GLA_HEADS, hv, hk), F32),
            pltpu.VMEM((tt // GLA_CHUNK, GLA_HEADS, GLA_CHUNK, GLA_CHUNK), BF16),
            pltpu.VMEM((tt, dk), BF16),
            pltpu.VMEM((tt, dk), BF16),
            pltpu.VMEM((tt // GLA_CHUNK, GLA_HEADS, SUBLANES, hk), F32),
            pltpu.VMEM((tt // GLA_CHUNK, expo.shape[0], dk), F32),
        ],
        compiler_params=_cparams("parallel", "arbitrary"),
        name="gla",
    )(proj, proj, proj, proj, log_a, g_gla, expo, masks)


def _pool_kernel(p_ref, w_ref, sc_ref, o_ref, *, seq, gw):
    row = lax.broadcasted_iota(jnp.int32, (seq, gw), 0)
    row1 = lax.broadcasted_iota(jnp.int32, (seq, 1), 0)
    for gi, w in enumerate(POOL_WINDOWS):
        cols = slice(gi * gw, (gi + 1) * gw)
        x = p_ref[:, cols].astype(F32)
        acc = x
        d = 1
        while d < w:
            acc = acc + jnp.where(row >= d, pltpu.roll(acc, d, 0), 0.0)
            d *= 2
        inv_cnt = 1.0 / jnp.minimum(row1 + 1, w).astype(F32)
        pooled = acc * inv_cnt - x
        mixed = _dot(pooled.astype(BF16), w_ref[gi]) * sc_ref[:, cols]
        o_ref[:, cols] = mixed.astype(o_ref.dtype)


def _pool(proj, w_pool, pool_scale, batch, seq):
    n = proj.shape[0]
    ng, gw, _ = w_pool.shape
    width = ng * gw
    return pl.pallas_call(
        functools.partial(_pool_kernel, seq=seq, gw=gw),
        grid=(batch,),
        in_specs=[
            pl.BlockSpec((seq, width), lambda b: (b, 0)),
            _const_spec(w_pool.shape),
            _const_spec((1, width)),
        ],
        out_specs=pl.BlockSpec((seq, width), lambda b: (b, 0)),
        out_shape=jax.ShapeDtypeStruct((n, width), BF16),
        compiler_params=_cparams("parallel"),
        name="pool",
    )(proj, w_pool, pool_scale)


def _merge_kernel(og_ref, op_ref, pg_ref, x_ref, wb_hbm, wo_hbm, gn_ref, x1_ref, h_ref,
                  wb_ref, wo_ref, stage, sem):
    @pl.when(pl.program_id(0) == 0)
    def _():
        _fetch_as_bf16(wb_hbm, wb_ref, stage, sem)
        _fetch_as_bf16(wo_hbm, wo_ref, stage, sem)

    dv = og_ref.shape[1]
    d = x_ref.shape[1]
    g0 = pg_ref.shape[1] - 2 * d
    yg = _dot(og_ref[...], wb_ref[0:dv, :])
    yp = _dot(op_ref[...], wb_ref[dv:, :])
    merged = (_sigmoid(pg_ref[:, g0:g0 + d].astype(F32)) * yg
              + _sigmoid(pg_ref[:, g0 + d:].astype(F32)) * yp)
    x1 = x_ref[...] + _dot(merged.astype(BF16), wo_ref[...])
    x1_ref[...] = x1
    h_ref[...] = _rms(x1, gn_ref[...]).astype(h_ref.dtype)


def _merge(o_gla, o_pool, proj_pg, x2d, w_branch, w_out, g_next, tm=256):
    n, d = x2d.shape
    dv = o_gla.shape[1]
    dp = o_pool.shape[1]
    wpg = proj_pg.shape[1]
    return pl.pallas_call(
        _merge_kernel,
        grid=(n // tm,),
        in_specs=[
            pl.BlockSpec((tm, dv), lambda i: (i, 0)),
            pl.BlockSpec((tm, dp), lambda i: (i, 0)),
            pl.BlockSpec((tm, wpg), lambda i: (i, 0)),
            pl.BlockSpec((tm, d), lambda i: (i, 0)),
            pl.BlockSpec(memory_space=pl.ANY),
            pl.BlockSpec(memory_space=pl.ANY),
            _const_spec((1, d)),
        ],
        out_specs=[
            pl.BlockSpec((tm, d), lambda i: (i, 0)),
            pl.BlockSpec((tm, d), lambda i: (i, 0)),
        ],
        out_shape=[
            jax.ShapeDtypeStruct((n, d), F32),
            jax.ShapeDtypeStruct((n, d), BF16),
        ],
        scratch_shapes=_resident_weight_scratch(w_branch, w_out),
        compiler_params=_cparams("arbitrary"),
        name="merge",
    )(o_gla, o_pool, proj_pg, x2d, w_branch, w_out, g_next)


def _norm_mm_kernel(x_ref, g_ref, w_ref, o_ref, wbf_ref):
    @pl.when(pl.program_id(1) == 0)
    def _():
        wbf_ref[...] = w_ref[...].astype(BF16)

    h = _rms(x_ref[...], g_ref[...]).astype(BF16)
    o_ref[...] = _dot(h, wbf_ref[...]).astype(o_ref.dtype)


def _norm_matmul(x2d, g, w, tm, tn, name):
    n, d = x2d.shape
    m = w.shape[1]
    return pl.pallas_call(
        _norm_mm_kernel,
        grid=(m // tn, n // tm),
        in_specs=[
            pl.BlockSpec((tm, d), lambda j, i: (i, 0)),
            _const_spec((1, d)),
            pl.BlockSpec((d, tn), lambda j, i: (0, j)),
        ],
        out_specs=pl.BlockSpec((tm, tn), lambda j, i: (i, j)),
        out_shape=jax.ShapeDtypeStruct((n, m), BF16),
        scratch_shapes=[pltpu.VMEM((d, tn), BF16)],
        compiler_params=_cparams("parallel", "arbitrary"),
        name=name,
    )(x2d, g, w)


def _cross_fold_kernel(kv_ref, wq_hbm, wo_hbm, wqk_ref, vw_ref, wq_ref, wo_ref, stage, sem):
    @pl.when(pl.program_id(0) == 0)
    def _():
        _fetch_as_bf16(wq_hbm, wq_ref, stage, sem)
        _fetch_as_bf16(wo_hbm, wo_ref, stage, sem)

    d = wq_ref.shape[0]
    n_mem = kv_ref.shape[0]
    hd = d // CROSS_HEADS
    for hh in range(CROSS_HEADS):
        cols = slice(hh * hd, (hh + 1) * hd)
        wqk = _dot_nt(wq_ref[:, cols], kv_ref[:, cols]) * (hd ** -0.5)
        wqk_ref[:, hh * n_mem:(hh + 1) * n_mem] = wqk.astype(wqk_ref.dtype)
        vw = _dot(kv_ref[:, d + hh * hd:d + (hh + 1) * hd], wo_ref[cols, :])
        vw_ref[hh * n_mem:(hh + 1) * n_mem, :] = vw.astype(vw_ref.dtype)


def _cross_fold(kv, w_cq, w_co, batch, n_mem):
    d = w_cq.shape[0]
    hm = CROSS_HEADS * n_mem
    return pl.pallas_call(
        _cross_fold_kernel,
        grid=(batch,),
        in_specs=[
            pl.BlockSpec((n_mem, 2 * d), lambda b: (b, 0)),
            pl.BlockSpec(memory_space=pl.ANY),
            pl.BlockSpec(memory_space=pl.ANY),
        ],
        out_specs=[
            pl.BlockSpec((d, hm), lambda b: (b, 0)),
            pl.BlockSpec((hm, d), lambda b: (b, 0)),
        ],
        out_shape=[
            jax.ShapeDtypeStruct((batch * d, hm), BF16),
            jax.ShapeDtypeStruct((batch * hm, d), BF16),
        ],
        scratch_shapes=_resident_weight_scratch(w_cq, w_co),
        compiler_params=_cparams("arbitrary"),
        name="cross_fold",
    )(kv, w_cq, w_co)


def _cross_kernel(h_ref, wqk_ref, vw_ref, x_ref, gn_ref, x2_ref, hn_ref, perm_ref):
    tm, d = h_ref.shape
    assert tm == FFN_BLOCK
    n_mem = wqk_ref.shape[1] // CROSS_HEADS
    s_all = _dot(h_ref[...], wqk_ref[...])
    probs = []
    for hh in range(CROSS_HEADS):
        s = s_all[:, hh * n_mem:(hh + 1) * n_mem]
        p = jnp.exp(s - jnp.max(s, axis=-1, keepdims=True))
        probs.append((p * (1.0 / jnp.sum(p, axis=-1, keepdims=True))).astype(BF16))
    x2 = x_ref[...] + _dot(jnp.concatenate(probs, axis=1), vw_ref[...])
    x2_ref[...] = x2
    hn = _rms(x2, gn_ref[...])
    for k in range(FFN_GROUPS):
        t0 = k * SUBLANES
        first = SUBLANES * (t0 % FFN_GROUPS) + t0 // FFN_GROUPS
        for c in range(perm_ref.shape[0]):
            perm_ref[c, pl.ds(first, SUBLANES, stride=SUBLANES), :] = (
                hn[t0:t0 + SUBLANES, c * LANES:(c + 1) * LANES])
    hn_ref[...] = jnp.concatenate([perm_ref[c] for c in range(perm_ref.shape[0])],
                                  axis=1).astype(hn_ref.dtype)


def _cross(h, wqk, vw, x2d, g_next, batch, seq, tm=FFN_BLOCK):
    n, d = x2d.shape
    hm = wqk.shape[1]
    spb = seq // tm
    return pl.pallas_call(
        _cross_kernel,
        grid=(batch, spb),
        in_specs=[
            pl.BlockSpec((tm, d), lambda b, i: (b * spb + i, 0)),
            pl.BlockSpec((d, hm), lambda b, i: (b, 0)),
            pl.BlockSpec((hm, d), lambda b, i: (b, 0)),
            pl.BlockSpec((tm, d), lambda b, i: (b * spb + i, 0)),
            _const_spec((1, d)),
        ],
        out_specs=[
            pl.BlockSpec((tm, d), lambda b, i: (b * spb + i, 0)),
            pl.BlockSpec((tm, d), lambda b, i: (b * spb + i, 0)),
        ],
        out_shape=[
            jax.ShapeDtypeStruct((n, d), F32),
            jax.ShapeDtypeStruct((n, d), BF16),
        ],
        scratch_shapes=[pltpu.VMEM((d // LANES, tm, LANES), F32)],
        compiler_params=_cparams("parallel", "parallel"),
        name="cross",
    )(h, wqk, vw, x2d, g_next)


def _ffn_up_kernel(h_ref, wg32_ref, wv32_ref, cwg_ref, cwv_ref, cbg_ref, cbv_ref, act_ref,
                   wg_ref, wv_ref, *, rb):
    seq, tn = act_ref.shape
    row = lax.broadcasted_iota(jnp.int32, (SUBLANES, tn), 0)

    @pl.when(pl.program_id(1) == 0)
    def _():
        wg_ref[...] = wg32_ref[...].astype(BF16)
        wv_ref[...] = wv32_ref[...].astype(BF16)

    def wrap(group, prev_group):
        return jnp.where(row == 0, prev_group[SUBLANES - 1:SUBLANES], pltpu.roll(group, 1, 0))

    def conv(u, prev, cw_ref, cb_ref):
        w30 = wrap(u[rb - 2 * SUBLANES:rb - SUBLANES], prev[:SUBLANES])
        w31 = wrap(u[rb - SUBLANES:], prev[SUBLANES:])
        u1 = jnp.concatenate([w31, u[:rb - SUBLANES]], axis=0)
        u2 = jnp.concatenate([w30, w31, u[:rb - 2 * SUBLANES]], axis=0)
        return cw_ref[2:3] * u + cw_ref[1:2] * u1 + cw_ref[0:1] * u2 + cb_ref[...]

    prev_g = jnp.zeros((2 * SUBLANES, tn), F32)
    prev_v = jnp.zeros((2 * SUBLANES, tn), F32)
    for r0 in range(0, seq, rb):
        hb = h_ref[r0:r0 + rb, :]
        ug = _dot(hb, wg_ref[...])
        uv = _dot(hb, wv_ref[...])
        gate = conv(ug, prev_g, cwg_ref, cbg_ref)
        val = conv(uv, prev_v, cwv_ref, cbv_ref)
        prev_g = ug[rb - 2 * SUBLANES:]
        prev_v = uv[rb - 2 * SUBLANES:]
        act_ref[r0:r0 + rb, :] = (gate * _sigmoid(gate) * val).astype(act_ref.dtype)


def _ffn_up(h, w_up, conv_w, conv_b, batch, seq, tn=512, rb=FFN_BLOCK):
    n, d = h.shape
    dff = w_up.shape[1] // 2
    nj = dff // tn
    return pl.pallas_call(
        functools.partial(_ffn_up_kernel, rb=rb),
        grid=(nj, batch),
        in_specs=[
            pl.BlockSpec((seq, d), lambda j, b: (b, 0)),
            pl.BlockSpec((d, tn), lambda j, b: (0, j)),
            pl.BlockSpec((d, tn), lambda j, b: (0, nj + j)),
            pl.BlockSpec((CONV_W, tn), lambda j, b: (0, j)),
            pl.BlockSpec((CONV_W, tn), lambda j, b: (0, nj + j)),
            pl.BlockSpec((1, tn), lambda j, b: (0, j)),
            pl.BlockSpec((1, tn), lambda j, b: (0, nj + j)),
        ],
        out_specs=pl.BlockSpec((seq, tn), lambda j, b: (b, j)),
        out_shape=jax.ShapeDtypeStruct((n, dff), BF16),
        scratch_shapes=[pltpu.VMEM((d, tn), BF16), pltpu.VMEM((d, tn), BF16)],
        compiler_params=_cparams("parallel", "arbitrary"),
        name="ffn_up",
    )(h, w_up, w_up, conv_w, conv_w, conv_b, conv_b)


def _ffn_down_kernel(a_ref, w_hbm, x_ref, g_ref, o_ref, y_ref, w_ref, stage, sem):
    assert a_ref.shape[0] == FFN_BLOCK

    @pl.when(pl.program_id(0) == 0)
    def _():
        _fetch_as_bf16(w_hbm, w_ref, stage, sem)

    _store_lane_chunks(y_ref, _dot(a_ref[...], w_ref[...]))
    for k in range(FFN_GROUPS):
        t0 = k * SUBLANES
        first = SUBLANES * (t0 % FFN_GROUPS) + t0 // FFN_GROUPS
        y = _load_strided_rows(y_ref, first, SUBLANES)
        rows = slice(t0, t0 + SUBLANES)
        o_ref[rows, :] = _rms(x_ref[rows, :] + y, g_ref[...])


def _ffn_down(act, w_down, x2d, g, tm=FFN_BLOCK):
    n, d = x2d.shape
    dff = act.shape[1]
    return pl.pallas_call(
        _ffn_down_kernel,
        grid=(n // tm,),
        in_specs=[
            pl.BlockSpec((tm, dff), lambda i: (i, 0)),
            pl.BlockSpec(memory_space=pl.ANY),
            pl.BlockSpec((tm, d), lambda i: (i, 0)),
            _const_spec((1, d)),
        ],
        out_specs=pl.BlockSpec((tm, d), lambda i: (i, 0)),
        out_shape=jax.ShapeDtypeStruct((n, d), F32),
        scratch_shapes=([pltpu.VMEM((d // LANES, tm, LANES), F32)]
                        + _resident_weight_scratch(w_down)),
        compiler_params=_cparams("arbitrary"),
        name="ffn_down",
    )(act, w_down, x2d, g)


def kernel(x, mem, g_mix, w_in, w_a2, b_a, g_gla, w_pool, pool_scale, w_branch, w_out,
           g_cross, g_mem, w_cq, w_ckv, w_co, g_ffn, w_up, conv_w, conv_b, w_down, g_final):
    batch, seq, d = x.shape
    n_mem = mem.shape[1]
    depth = w_in.shape[0]
    assert depth == 1
    dk = w_a2.shape[2]
    dv = g_gla.shape[1]
    off_a = 2 * dk + 2 * dv
    off_p = off_a + GLA_GATE_RANK
    row2 = lambda v: v.reshape(1, -1)

    xs = x.reshape(batch * seq, d)
    mem2d = mem.reshape(batch * n_mem, d)

    w_in_t = jnp.swapaxes(w_in[0], 0, 1)
    d_in = w_in_t.shape[0]
    wa2 = jnp.pad(w_a2[0], ((0, LANES - GLA_GATE_RANK), (0, 0))).astype(BF16)

    h, log_a = _norm_gate(xs, row2(g_mix[0]), w_in_t, off_a, wa2, row2(b_a[0]))
    proj_qkvr = _matmul_nt(h, w_in_t, 0, off_a, 1024, 1024, BF16, "in_proj_qkvr")
    proj_pg = _matmul_nt(h, w_in_t, off_p, d_in - off_p, 1024, 1024, BF16, "in_proj_pg")
    o_gla = _gla(proj_qkvr, log_a, row2(g_gla[0]), batch, seq, dk, dv)
    o_pool = _pool(proj_pg, w_pool[0].astype(BF16), row2(pool_scale[0]), batch, seq)
    x1, h2 = _merge(o_gla, o_pool, proj_pg, xs, w_branch[0], w_out[0],
                    row2(g_cross[0]))
    kv = _norm_matmul(mem2d, row2(g_mem[0]), w_ckv[0], 1024, 1024, "mem_kv")
    wqk, vw = _cross_fold(kv, w_cq[0], w_co[0], batch, n_mem)
    x2, h3 = _cross(h2, wqk, vw, x1, row2(g_ffn[0]), batch, seq)
    act = _ffn_up(h3, w_up[0], conv_w[0], row2(conv_b[0]), batch, seq)
    out = _ffn_down(act, w_down[0], x2, row2(g_final))
    return out.reshape(batch, seq, d)
```
